```python
import jax, jax.numpy as jnp
from jax import lax
import numpy as np

D_MODEL = 1024
BATCH = 4
SEQ = 8192
DEPTH = 4

N_EVEN = (DEPTH + 1) // 2
N_ODD = DEPTH // 2
D_CONV_BR = D_MODEL // 2
CONF_KERNEL = 31
D_HGRN = D_MODEL // 2
HGRN_HEADS = 4
HGRN_DK = D_HGRN // HGRN_HEADS
HGRN_DV = D_HGRN // HGRN_HEADS
HGRN_CHUNK = 64
D_IN = 2 * D_CONV_BR + 4 * D_HGRN
SPLITS = (D_CONV_BR, 2 * D_CONV_BR, 2 * D_CONV_BR + D_HGRN, 2 * D_CONV_BR + 2 * D_HGRN, 2 * D_CONV_BR + 3 * D_HGRN)
RWKV_HEAD = 64
RWKV_HEADS = D_MODEL // RWKV_HEAD
LORA_DECAY = 64
LORA_A = 64
LORA_GATE = 160
GN_EPS = 64e-5
D_FF = 2816
FFN_KERNEL = 3
ALPHA = (2 * DEPTH) ** 0.25
BETA = (8 * DEPTH) ** -0.25
LN_EPS = 1e-5
F32 = jnp.float32

kernel_name = 'hybrid_conv_hgrn2_rwkv7_deepnorm'


def layer_norm(x, g, b, eps=LN_EPS):
    xf = x.astype(F32)
    mu = jnp.mean(xf, -1, keepdims=True)
    var = jnp.mean(jnp.square(xf - mu), -1, keepdims=True)
    return ((xf - mu) * lax.rsqrt(var + eps)).astype(x.dtype) * g + b


def causal_dwconv(x, w, b):
    K, C = w.shape
    y = lax.conv_general_dilated(x, w[:, None, :].astype(x.dtype), window_strides=(1,),
                                 padding=((K - 1, 0),), dimension_numbers=('NWC', 'WIO', 'NWC'),
                                 feature_group_count=C)
    return y + b


def hgrn2_chunked(q, fz, v, lb):
    Bsz, T, _ = q.shape
    nc = T // HGRN_CHUNK
    lbf = lb.astype(F32)
    fzf = fz.astype(F32)
    qf = jax.nn.silu(q.astype(F32))
    log_f = jnp.logaddexp(jnp.log(lbf), jnp.log1p(-lbf) + jax.nn.log_sigmoid(fzf))
    kf = (1.0 - lbf) * jax.nn.sigmoid(-fzf)

    def to_chunks(t, d):
        return t.reshape(Bsz, nc, HGRN_CHUNK, HGRN_HEADS, d).transpose(1, 0, 3, 2, 4)

    qc, lfc, kc = to_chunks(qf, HGRN_DK), to_chunks(log_f, HGRN_DK), to_chunks(kf, HGRN_DK)
    vc = to_chunks(v.astype(F32), HGRN_DV)
    causal = jnp.tril(jnp.ones((HGRN_CHUNK, HGRN_CHUNK), bool))[:, :, None]

    def step(S, inp):
        qb, lf, kb, vb = inp
        cum = jnp.cumsum(lf, axis=2)
        rel = jnp.where(causal, cum[:, :, :, None, :] - cum[:, :, None, :, :], -jnp.inf)
        scores = jnp.einsum('bhtk,bhsk,bhtsk->bhts', qb, kb, jnp.exp(rel))
        o = (jnp.einsum('bhts,bhsv->bhtv', scores, vb)
             + jnp.einsum('bhtk,bhkv->bhtv', qb * jnp.exp(cum), S))
        last = cum[:, :, -1:, :]
        S = (jnp.exp(last[:, :, 0, :])[..., None] * S
             + jnp.einsum('bhsk,bhsv->bhkv', kb * jnp.exp(last - cum), vb))
        return S, o

    S0 = jnp.zeros((Bsz, HGRN_HEADS, HGRN_DK, HGRN_DV), F32)
    _, oc = lax.scan(step, S0, (qc, lfc, kc, vc))
    return oc.transpose(1, 0, 3, 2, 4).reshape(Bsz, T, HGRN_HEADS, HGRN_DV)


def conv_hgrn_mixer(x, w_in, b_in, conv_w, conv_b, cln_g, cln_b, lb, onorm_g, w_out):
    z = x @ w_in + b_in
    a_val, a_gate, q, fz, iv, gate = jnp.split(z, SPLITS, axis=-1)
    ua = causal_dwconv(a_val * jax.nn.sigmoid(a_gate), conv_w, conv_b)
    ua = jax.nn.silu(layer_norm(ua, cln_g, cln_b))
    ob = hgrn2_chunked(q, fz, iv, lb)
    ob = ob * lax.rsqrt(jnp.mean(jnp.square(ob), -1, keepdims=True) + 1e-6)
    ob = ob.reshape(x.shape[0], x.shape[1], D_HGRN).astype(x.dtype) * onorm_g * jax.nn.silu(gate)
    return jnp.concatenate([ua, ob], axis=-1) @ w_out


def rwkv7_mixer(x, mu, w_r, w_k, w_v, w_o, w0, w1, w2, a0, a1, a2, g1, g2, k_k, k_a, r_k, gn_g, gn_b):
    Bsz, T, D = x.shape
    xx = jnp.pad(x, ((0, 0), (1, 0), (0, 0)))[:, :-1] - x
    xr, xw, xk, xv, xa, xg = (x + xx * mu[i] for i in range(6))
    r = xr @ w_r
    k = xk @ w_k
    v = xv @ w_v
    w_log = -jax.nn.softplus(-(w0 + jnp.tanh(xw @ w1) @ w2)) - 0.5
    a = jax.nn.sigmoid(a0 + (xa @ a1) @ a2)
    g = jax.nn.sigmoid(xg @ g1) @ g2

    def heads(t):
        return t.reshape(Bsz, T, RWKV_HEADS, RWKV_HEAD).astype(F32)

    kk = heads(k * k_k)
    kk = kk / jnp.maximum(jnp.sqrt(jnp.sum(jnp.square(kk), -1, keepdims=True)), 1e-12)
    k = k * (1.0 + (a - 1.0) * k_a)
    rf, kf, vf, af = heads(r), heads(k), heads(v), heads(a)
    decay = jnp.exp(-jnp.exp(heads(w_log)))

    def step(S, inp):
        r_t, w_t, k_t, v_t, kk_t, a_t = inp
        sa = jnp.einsum('bhvk,bhk->bhv', S, -kk_t)
        S = (S * w_t[:, :, None, :] + sa[..., None] * (kk_t * a_t)[:, :, None, :]
             + v_t[..., None] * k_t[:, :, None, :])
        return S, jnp.einsum('bhvk,bhk->bhv', S, r_t)

    xs = tuple(t.transpose(1, 0, 2, 3) for t in (rf, decay, kf, vf, kk, af))
    S0 = jnp.zeros((Bsz, RWKV_HEADS, RWKV_HEAD, RWKV_HEAD), F32)
    _, y = lax.scan(step, S0, xs)
    y = y.transpose(1, 0, 2, 3)
    ym = jnp.mean(y, -1, keepdims=True)
    yv = jnp.mean(jnp.square(y - ym), -1, keepdims=True)
    yn = ((y - ym) * lax.rsqrt(yv + GN_EPS)).reshape(Bsz, T, D).astype(x.dtype) * gn_g + gn_b
    bonus = (jnp.sum(rf * kf * r_k, -1, keepdims=True) * vf).reshape(Bsz, T, D).astype(x.dtype)
    return ((yn + bonus) * g) @ w_o


def conv_ffn(x, w_up, w_gate, conv_w, conv_b, w_down):
    u = causal_dwconv(x @ w_up, conv_w, conv_b)
    return (jax.nn.gelu(u, approximate=False) * (x @ w_gate)) @ w_down


def setup_inputs(seed: int = 0) -> dict:
    key = jax.random.key(seed)
    ks = iter(jax.random.split(key, 48))

    def nrm(shape, scale):
        return jax.random.normal(next(ks), shape, F32) * scale

    def gain(shape):
        return 1.0 + nrm(shape, 0.02)

    D = D_MODEL
    return {
        'x': nrm((BATCH, SEQ, D), 1.0),
        'ln_mix_g': gain((DEPTH, D)),
        'ln_mix_b': nrm((DEPTH, D), 0.02),
        'ln_ffn_g': gain((DEPTH, D)),
        'ln_ffn_b': nrm((DEPTH, D), 0.02),
        'ev_w_in': nrm((N_EVEN, D, D_IN), D ** -0.5),
        'ev_b_in': nrm((N_EVEN, D_IN), 0.02),
        'ev_conv_w': nrm((N_EVEN, CONF_KERNEL, D_CONV_BR), CONF_KERNEL ** -0.5),
        'ev_conv_b': nrm((N_EVEN, D_CONV_BR), 0.02),
        'ev_cln_g': gain((N_EVEN, D_CONV_BR)),
        'ev_cln_b': nrm((N_EVEN, D_CONV_BR), 0.02),
        'ev_lb_logits': nrm((N_EVEN, D_HGRN), 0.5),
        'ev_onorm_g': gain((N_EVEN, D_HGRN)),
        'ev_w_out': nrm((N_EVEN, D, D), BETA * D ** -0.5),
        'rw_mu': jax.random.uniform(next(ks), (N_ODD, 6, D), F32),
        'rw_w_r': nrm((N_ODD, D, D), D ** -0.5),
        'rw_w_k': nrm((N_ODD, D, D), D ** -0.5),
        'rw_w_v': nrm((N_ODD, D, D), D ** -0.5),
        'rw_w_o': nrm((N_ODD, D, D), BETA * D ** -0.5),
        'rw_w0': jnp.linspace(-6.0, 1.0, D, dtype=F32)[None, :] + nrm((N_ODD, D), 0.1),
        'rw_w1': nrm((N_ODD, D, LORA_DECAY), D ** -0.5),
        'rw_w2': nrm((N_ODD, LORA_DECAY, D), LORA_DECAY ** -0.5),
        'rw_a0': nrm((N_ODD, D), 0.1),
        'rw_a1': nrm((N_ODD, D, LORA_A), D ** -0.5),
        'rw_a2': nrm((N_ODD, LORA_A, D), LORA_A ** -0.5),
        'rw_g1': nrm((N_ODD, D, LORA_GATE), D ** -0.5),
        'rw_g2': nrm((N_ODD, LORA_GATE, D), LORA_GATE ** -0.5),
        'rw_k_k': 0.85 + nrm((N_ODD, D), 0.02),
        'rw_k_a': gain((N_ODD, D)),
        'rw_r_k': nrm((N_ODD, RWKV_HEADS, RWKV_HEAD), 0.1),
        'rw_gn_g': gain((N_ODD, D)),
        'rw_gn_b': nrm((N_ODD, D), 0.02),
        'ff_w_up': nrm((DEPTH, D, D_FF), D ** -0.5),
        'ff_w_gate': nrm((DEPTH, D, D_FF), D ** -0.5),
        'ff_conv_w': nrm((DEPTH, FFN_KERNEL, D_FF), FFN_KERNEL ** -0.5),
        'ff_conv_b': nrm((DEPTH, D_FF), 0.02),
        'ff_w_down': nrm((DEPTH, D_FF, D), BETA * D_FF ** -0.5),
    }


def reference(x, ln_mix_g, ln_mix_b, ln_ffn_g, ln_ffn_b,
              ev_w_in, ev_b_in, ev_conv_w, ev_conv_b, ev_cln_g, ev_cln_b, ev_lb_logits, ev_onorm_g, ev_w_out,
              rw_mu, rw_w_r, rw_w_k, rw_w_v, rw_w_o, rw_w0, rw_w1, rw_w2, rw_a0, rw_a1, rw_a2,
              rw_g1, rw_g2, rw_k_k, rw_k_a, rw_r_k, rw_gn_g, rw_gn_b,
              ff_w_up, ff_w_gate, ff_conv_w, ff_conv_b, ff_w_down):
    lb_all = jnp.cumsum(jax.nn.softmax(ev_lb_logits.astype(F32), axis=0), axis=0)
    lb_all = lb_all - lb_all[0]
    for layer in range(DEPTH):
        j = layer // 2
        if layer % 2 == 0:
            y = conv_hgrn_mixer(x, ev_w_in[j], ev_b_in[j], ev_conv_w[j], ev_conv_b[j],
                                ev_cln_g[j], ev_cln_b[j], lb_all[j], ev_onorm_g[j], ev_w_out[j])
        else:
            y = rwkv7_mixer(x, rw_mu[j], rw_w_r[j], rw_w_k[j], rw_w_v[j], rw_w_o[j],
                            rw_w0[j], rw_w1[j], rw_w2[j], rw_a0[j], rw_a1[j], rw_a2[j],
                            rw_g1[j], rw_g2[j], rw_k_k[j], rw_k_a[j], rw_r_k[j], rw_gn_g[j], rw_gn_b[j])
        x = layer_norm(ALPHA * x + y, ln_mix_g[layer], ln_mix_b[layer])
        f = conv_ffn(x, ff_w_up[layer], ff_w_gate[layer], ff_conv_w[layer], ff_conv_b[layer], ff_w_down[layer])
        x = layer_norm(ALPHA * x + f, ln_ffn_g[layer], ln_ffn_b[layer])
    return x
```

```python
import functools

import jax
import jax.numpy as jnp
from jax import lax
from jax.experimental import pallas as pl
from jax.experimental.pallas import tpu as pltpu

F32 = jnp.float32
BF16 = jnp.bfloat16

HGRN_HEAD = 128
RWKV_HEAD = 64
LN_EPS = 1e-5
GN_EPS = 64e-5
RMS_EPS = 1e-6
KK_EPS = 1e-12

LANES = 128
SUBLANES = 8
VMEM_LIMIT_BYTES = 56 * 1024 * 1024

ROW_TILE = 512
FF_CHUNK = 256
HGRN_SUB = 16
HGRN_ROWS = 128
RWKV_CHUNK = 64
RWKV_ROWS = 256
CONV_HALO = 32


def _cparams(*sem):
    return pltpu.CompilerParams(dimension_semantics=sem, vmem_limit_bytes=VMEM_LIMIT_BYTES)


def _dot(a, b):
    return jnp.dot(a, b, preferred_element_type=F32)


def _dot_nt(a, b):
    return lax.dot_general(a, b, (((1,), (1,)), ((), ())), preferred_element_type=F32)


def _dot_tn(a, b):
    return lax.dot_general(a, b, (((0,), (0,)), ((), ())), preferred_element_type=F32)


def _split3(x):
    hi = x.astype(BF16)
    r1 = x - hi.astype(F32)
    mid = r1.astype(BF16)
    lo = (r1 - mid.astype(F32)).astype(BF16)
    return hi, mid, lo


def _dot01_left(m01, x):
    hi, mid, lo = _split3(x)
    return _dot(m01, hi) + _dot(m01, mid) + _dot(m01, lo)


def _dot01_right(x, m01):
    hi, mid, lo = _split3(x)
    return _dot(hi, m01) + _dot(mid, m01) + _dot(lo, m01)


def _sigmoid(x):
    return 1.0 / (1.0 + jnp.exp(-x))


def _layer_norm(y, g, b):
    mu = jnp.mean(y, axis=-1, keepdims=True)
    yc = y - mu
    var = jnp.mean(yc * yc, axis=-1, keepdims=True)
    return yc * lax.rsqrt(var + LN_EPS) * g + b


def _shift_rows(u, prev, k):
    rows = u.shape[0]
    row = lax.broadcasted_iota(jnp.int32, (rows, 1), 0)
    out = pltpu.roll(u, k, 0)
    for j in range(k):
        out = jnp.where(row == j, prev[SUBLANES - k + j:SUBLANES - k + j + 1], out)
    return out


def _const_spec(shape):
    return pl.BlockSpec(shape, lambda *_: (0,) * len(shape))


def _halo_index(i, tm, halo):
    return jnp.maximum(i * (tm // halo) - 1, 0)


def _ffn_kernel(x_ref, xh_ref, wup_ref, wgate_ref, cw_ref, cb_ref, wdown_ref, g_ref, b_ref, o_ref, h_ref,
                *, tm, seq, alpha):
    i = pl.program_id(0)
    x = x_ref[...]
    xb = x.astype(BF16)
    xhb = xh_ref[...].astype(BF16)
    at_seq_start = (i * tm) % seq == 0
    dff = wup_ref.shape[1]
    for c in range(dff // FF_CHUNK):
        sl = slice(c * FF_CHUNK, (c + 1) * FF_CHUNK)
        wu = wup_ref[:, sl]
        u = _dot(xb, wu)
        uh = jnp.where(at_seq_start, 0.0, _dot(xhb, wu))
        cw = cw_ref[:, sl]
        conv = cw[2:3] * u + cw[1:2] * _shift_rows(u, uh, 1) + cw[0:1] * _shift_rows(u, uh, 2) + cb_ref[:, sl]
        gelu = 0.5 * conv * (1.0 + lax.erf(conv * (0.5 ** 0.5)))
        h_ref[:, sl] = (gelu * _dot(xb, wgate_ref[:, sl])).astype(BF16)
    f = _dot(h_ref[...], wdown_ref[...])
    o_ref[...] = _layer_norm(alpha * x + f, g_ref[...], b_ref[...])


def _ffn(x, w_up, w_gate, conv_w, conv_b, w_down, ln_g, ln_b, *, seq, alpha):
    n, d = x.shape
    dff = w_up.shape[1]
    tm = min(ROW_TILE, seq)
    return pl.pallas_call(
        functools.partial(_ffn_kernel, tm=tm, seq=seq, alpha=alpha),
        grid=(n // tm,),
        in_specs=[
            pl.BlockSpec((tm, d), lambda i: (i, 0)),
            pl.BlockSpec((SUBLANES, d), lambda i: (_halo_index(i, tm, SUBLANES), 0)),
            _const_spec((d, dff)), _const_spec((d, dff)), _const_spec(conv_w.shape), _const_spec((1, dff)),
            _const_spec((dff, d)), _const_spec((1, d)), _const_spec((1, d)),
        ],
        out_specs=pl.BlockSpec((tm, d), lambda i: (i, 0)),
        out_shape=jax.ShapeDtypeStruct((n, d), F32),
        scratch_shapes=[pltpu.VMEM((tm, dff), BF16)],
        compiler_params=_cparams("parallel"),
        name="conv_ffn",
    )(x, x, w_up, w_gate, conv_w, conv_b, w_down, ln_g, ln_b)


def _ev_in_kernel(x_ref, w_ref, b_ref, lbl_ref, glu_ref, q_ref, lf_ref, k_ref, v_ref, sg_ref, *, layer_j):
    xb = x_ref[...].astype(BF16)
    width = glu_ref.shape[1]

    def proj(c):
        sl = slice(c * width, (c + 1) * width)
        return _dot(xb, w_ref[:, sl]) + b_ref[:, sl]

    glu_ref[...] = proj(0) * _sigmoid(proj(1))
    q = proj(2)
    q_ref[...] = q * _sigmoid(q)

    logits = lbl_ref[...]
    e = jnp.exp(logits - jnp.max(logits, axis=0, keepdims=True))
    p = e / jnp.sum(e, axis=0, keepdims=True)
    lb = jnp.zeros_like(p[0:1])
    for r in range(1, layer_j + 1):
        lb = lb + p[r:r + 1]

    fz = proj(3)
    en = jnp.exp(-jnp.abs(fz))
    inv = 1.0 / (1.0 + en)
    log_sig = jnp.minimum(fz, 0.0) - jnp.log1p(en)
    sig_neg = jnp.where(fz >= 0.0, en * inv, inv)
    a = jnp.log(lb)
    bterm = jnp.log1p(-lb) + log_sig
    lf_ref[...] = jnp.maximum(a, bterm) + jnp.log1p(jnp.exp(-jnp.abs(a - bterm)))
    k_ref[...] = (1.0 - lb) * sig_neg
    v_ref[...] = proj(4)
    gate = proj(5)
    sg_ref[...] = gate * _sigmoid(gate)


def _ev_in(x, w_in, b_in, lb_logits, *, layer_j, seq):
    n, d = x.shape
    din = w_in.shape[1]
    width = din // 6
    tm = min(ROW_TILE, seq)
    out = jax.ShapeDtypeStruct((n, width), F32)
    return pl.pallas_call(
        functools.partial(_ev_in_kernel, layer_j=layer_j),
        grid=(n // tm,),
        in_specs=[pl.BlockSpec((tm, d), lambda i: (i, 0)), _const_spec((d, din)), _const_spec((1, din)),
                  _const_spec(lb_logits.shape)],
        out_specs=[pl.BlockSpec((tm, width), lambda i: (i, 0))] * 6,
        out_shape=[out] * 6,
        compiler_params=_cparams("parallel"),
        name="ev_in_proj",
    )(x, w_in, b_in, lb_logits)


def _hgrn_kernel(q_ref, lf_ref, k_ref, v_ref, sg_ref, g_ref, o_ref, st_ref, *, rows):
    @pl.when(pl.program_id(2) == 0)
    def _():
        st_ref[...] = jnp.zeros_like(st_ref)

    sub = HGRN_SUB
    r_i = lax.broadcasted_iota(jnp.int32, (rows, rows), 0)
    c_i = lax.broadcasted_iota(jnp.int32, (rows, rows), 1)
    same = (r_i // sub) == (c_i // sub)
    tri = jnp.where(same & (c_i <= r_i), 1.0, 0.0).astype(BF16)
    blk = jnp.where(same, 1.0, 0.0).astype(BF16)

    lf = lf_ref[...]
    lc = _dot01_left(tri, lf)
    tot = _dot01_left(blk, lf)
    q = q_ref[...]
    k = k_ref[...]
    v = v_ref[...]
    q_in = (q * jnp.exp(lc)).astype(BF16)
    k_out = (k * jnp.exp(tot - lc)).astype(BF16)
    dec = jnp.exp(tot)
    vb = v.astype(BF16)

    t_i = lax.broadcasted_iota(jnp.int32, (sub, 1), 0)
    st = st_ref[...]
    outs = []
    for i in range(rows // sub):
        rs = slice(i * sub, (i + 1) * sub)
        o = _dot_nt(q_in[rs], st.astype(BF16))
        lci, qi, ki, vi = lc[rs], q[rs], k[rs], v[rs]
        for s in range(sub):
            e = jnp.exp(jnp.minimum(lci - lci[s:s + 1], 0.0))
            col = jnp.sum(qi * (ki[s:s + 1] * e), axis=-1, keepdims=True)
            o = o + jnp.where(t_i >= s, col, 0.0) * vi[s:s + 1]
        outs.append(o)
        st = st * dec[i * sub:i * sub + 1] + _dot_tn(vb[rs], k_out[rs])
    st_ref[...] = st

    o = jnp.concatenate(outs, axis=0)
    o = o * lax.rsqrt(jnp.mean(o * o, axis=-1, keepdims=True) + RMS_EPS)
    o_ref[...] = o * g_ref[...] * sg_ref[...]


def _hgrn(q, lf, k, v, sg, onorm_g, *, batch, seq):
    n, width = q.shape
    heads = width // HGRN_HEAD
    rows = min(HGRN_ROWS, seq)
    steps = seq // rows
    blk = pl.BlockSpec((rows, HGRN_HEAD), lambda b, h, t: (b * steps + t, h))
    return pl.pallas_call(
        functools.partial(_hgrn_kernel, rows=rows),
        grid=(batch, heads, steps),
        in_specs=[blk] * 5 + [pl.BlockSpec((1, HGRN_HEAD), lambda b, h, t: (0, h))],
        out_specs=blk,
        out_shape=jax.ShapeDtypeStruct((n, width), F32),
        scratch_shapes=[pltpu.VMEM((HGRN_HEAD, HGRN_HEAD), F32)],
        compiler_params=_cparams("parallel", "parallel", "arbitrary"),
        name="hgrn2",
    )(q, lf, k, v, sg, onorm_g)


def _ev_out_kernel(x_ref, glu_ref, gh_ref, ob_ref, cw_ref, cb_ref, cg_ref, cbeta_ref, wout_ref, g_ref, b_ref,
                   o_ref, gs_ref, *, tm, seq, alpha):
    i = pl.program_id(0)
    at_seq_start = (i * tm) % seq == 0
    taps, width = cw_ref.shape
    gs_ref[0:CONV_HALO, :] = jnp.where(at_seq_start, 0.0, gh_ref[...])
    gs_ref[CONV_HALO:CONV_HALO + tm, :] = glu_ref[...]
    acc = jnp.zeros((tm, width), F32) + cb_ref[...]
    first = CONV_HALO - (taps - 1)
    for j in range(taps):
        acc = acc + cw_ref[j:j + 1, :] * gs_ref[first + j:first + j + tm, :]
    ua = _layer_norm(acc, cg_ref[...], cbeta_ref[...])
    ua = ua * _sigmoid(ua)
    y = _dot(ua.astype(BF16), wout_ref[0:width, :]) + _dot(ob_ref[...].astype(BF16), wout_ref[width:, :])
    o_ref[...] = _layer_norm(alpha * x_ref[...] + y, g_ref[...], b_ref[...])


def _ev_out(x, glu, ob, conv_w, conv_b, cln_g, cln_b, w_out, ln_g, ln_b, *, seq, alpha):
    n, d = x.shape
    width = glu.shape[1]
    tm = min(ROW_TILE, seq)
    return pl.pallas_call(
        functools.partial(_ev_out_kernel, tm=tm, seq=seq, alpha=alpha),
        grid=(n // tm,),
        in_specs=[
            pl.BlockSpec((tm, d), lambda i: (i, 0)),
            pl.BlockSpec((tm, width), lambda i: (i, 0)),
            pl.BlockSpec((CONV_HALO, width), lambda i: (_halo_index(i, tm, CONV_HALO), 0)),
            pl.BlockSpec((tm, width), lambda i: (i, 0)),
            _const_spec(conv_w.shape), _const_spec((1, width)), _const_spec((1, width)), _const_spec((1, width)),
            _const_spec((d, d)), _const_spec((1, d)), _const_spec((1, d)),
        ],
        out_specs=pl.BlockSpec((tm, d), lambda i: (i, 0)),
        out_shape=jax.ShapeDtypeStruct((n, d), F32),
        scratch_shapes=[pltpu.VMEM((CONV_HALO + tm, width), F32)],
        compiler_params=_cparams("parallel"),
        name="ev_out_proj",
    )(x, glu, glu, ob, conv_w, conv_b, cln_g, cln_b, w_out, ln_g, ln_b)


def _rw_in_kernel(x_ref, xh_ref, mu_ref, wr_ref, wk_ref, wv_ref, w0_ref, w1_ref, w2_ref, a0_ref, a1_ref, a2_ref,
                  g1_ref, g2_ref, r_ref, k_ref, v_ref, lw_ref, a_ref, g_ref, *, tm, seq):
    i = pl.program_id(0)
    x = x_ref[...]
    prev = jnp.where((i * tm) % seq == 0, 0.0, xh_ref[...])
    xx = _shift_rows(x, prev, 1) - x

    def mix(j):
        return (x + xx * mu_ref[j:j + 1, :]).astype(BF16)

    r_ref[...] = _dot(mix(0), wr_ref[...])
    z = w0_ref[...] + _dot(jnp.tanh(_dot(mix(1), w1_ref[...])).astype(BF16), w2_ref[...])
    lw_ref[...] = -(_sigmoid(z) * (jnp.exp(-0.5)))
    k_ref[...] = _dot(mix(2), wk_ref[...])
    v_ref[...] = _dot(mix(3), wv_ref[...])
    a_ref[...] = _sigmoid(a0_ref[...] + _dot(_dot(mix(4), a1_ref[...]).astype(BF16), a2_ref[...]))
    g_ref[...] = _dot(_sigmoid(_dot(mix(5), g1_ref[...])).astype(BF16), g2_ref[...])


def _rw_in(x, mu, w_r, w_k, w_v, w0, w1, w2, a0, a1, a2, g1, g2, *, seq):
    n, d = x.shape
    tm = min(ROW_TILE, seq)
    out = jax.ShapeDtypeStruct((n, d), F32)
    consts = (mu, w_r, w_k, w_v, w0, w1, w2, a0, a1, a2, g1, g2)
    return pl.pallas_call(
        functools.partial(_rw_in_kernel, tm=tm, seq=seq),
        grid=(n // tm,),
        in_specs=[pl.BlockSpec((tm, d), lambda i: (i, 0)),
                  pl.BlockSpec((SUBLANES, d), lambda i: (_halo_index(i, tm, SUBLANES), 0))]
                 + [_const_spec(c.shape) for c in consts],
        out_specs=[pl.BlockSpec((tm, d), lambda i: (i, 0))] * 6,
        out_shape=[out] * 6,
        compiler_params=_cparams("parallel"),
        name="rwkv_in_proj",
    )(x, x, *consts)


def _rwkv_kernel(r_ref, k_ref, v_ref, lw_ref, a_ref, g_ref, kk_ref, ka_ref, rk_ref, gng_ref, gnb_ref, o_ref,
                 st_ref, *, rows):
    @pl.when(pl.program_id(2) == 0)
    def _():
        st_ref[...] = jnp.zeros_like(st_ref)

    ch = RWKV_CHUNK
    hd = RWKV_HEAD
    width = 2 * hd
    lane = lax.broadcasted_iota(jnp.int32, (1, width), 1)
    m0 = jnp.where(lane < hd, 1.0, 0.0)
    m1 = 1.0 - m0
    r_i = lax.broadcasted_iota(jnp.int32, (width, width), 0)
    c_i = lax.broadcasted_iota(jnp.int32, (width, width), 1)
    same_head = (r_i // hd) == (c_i // hd)
    head_ones = jnp.where(same_head, 1.0, 0.0).astype(BF16)
    bd_mask = jnp.where(same_head, 1.0, 0.0)
    tt = lax.broadcasted_iota(jnp.int32, (rows, rows), 0)
    ss = lax.broadcasted_iota(jnp.int32, (rows, rows), 1)
    same_chunk = (tt // ch) == (ss // ch)
    tri = jnp.where(same_chunk & (ss <= tt), 1.0, 0.0).astype(BF16)
    blk = jnp.where(same_chunk, 1.0, 0.0).astype(BF16)
    p_r = lax.broadcasted_iota(jnp.int32, (2 * ch, 2 * ch), 0) % ch
    p_c = lax.broadcasted_iota(jnp.int32, (2 * ch, 2 * ch), 1) % ch
    strict = p_r > p_c
    incl = p_r >= p_c
    eye = jnp.where(lax.broadcasted_iota(jnp.int32, (2 * ch, 2 * ch), 0)
                    == lax.broadcasted_iota(jnp.int32, (2 * ch, 2 * ch), 1), 1.0, 0.0)

    r = r_ref[...]
    k = k_ref[...]
    v = v_ref[...]
    lw = lw_ref[...]
    a = a_ref[...]

    kkr = k * kk_ref[...]
    n2 = _dot01_right(kkr * kkr, head_ones)
    kk = kkr / jnp.maximum(jnp.sqrt(n2), KK_EPS)
    kp = k * (1.0 + (a - 1.0) * ka_ref[...])
    beta = kk * a

    cum = _dot01_left(tri, lw)
    tot = _dot01_left(blk, lw)
    e_cum = jnp.exp(cum)
    e_neg = jnp.exp(-cum)
    e_out = jnp.exp(tot - cum)
    r_t = r * e_cum
    k_t = kp * e_neg
    a_t = -kk * jnp.exp(cum - lw)
    b_t = beta * e_neg
    k_o = kp * e_out
    b_o = beta * e_out
    dec = jnp.exp(tot)

    def stack(x):
        return jnp.concatenate([x * m0, x * m1], axis=0)

    def fold(x):
        return x[0:ch] + x[ch:2 * ch]

    gst = st_ref[...]
    ys = []
    for c in range(rows // ch):
        rs = slice(c * ch, (c + 1) * ch)
        lhs = jnp.concatenate([stack(a_t[rs]), stack(r_t[rs])], axis=0).astype(BF16)
        rhs = jnp.concatenate([stack(k_t[rs]), stack(b_t[rs])], axis=0).astype(BF16)
        p = _dot_nt(lhs, rhs)
        l_ak = jnp.where(strict, p[0:2 * ch, 0:2 * ch], 0.0)
        l_ab = jnp.where(strict, p[0:2 * ch, 2 * ch:], 0.0)
        a_rk = jnp.where(incl, p[2 * ch:, 0:2 * ch], 0.0)
        a_rb = jnp.where(incl, p[2 * ch:, 2 * ch:], 0.0)

        tinv = eye + l_ab
        lp = l_ab
        n_sq = ch.bit_length() - 2
        for _ in range(n_sq):
            lp = jnp.dot(lp, lp, preferred_element_type=F32, precision=lax.Precision.HIGHEST)
            tinv = tinv + jnp.dot(tinv, lp, preferred_element_type=F32, precision=lax.Precision.HIGHEST)

        gb = gst.astype(BF16)
        vs = stack(v[rs]).astype(BF16)
        w = _dot_nt(a_t[rs].astype(BF16), gb) + _dot(fold(l_ak).astype(BF16), vs)
        u = jnp.dot(fold(tinv), stack(w), preferred_element_type=F32, precision=lax.Precision.HIGHEST)
        y = (_dot_nt(r_t[rs].astype(BF16), gb) + _dot(fold(a_rk).astype(BF16), vs)
             + _dot(fold(a_rb).astype(BF16), stack(u).astype(BF16)))
        ys.append(y)
        vu = jnp.concatenate([v[rs], u], axis=0).astype(BF16)
        kb = jnp.concatenate([k_o[rs], b_o[rs]], axis=0).astype(BF16)
        gst = gst * dec[c * ch:c * ch + 1] + bd_mask * _dot_tn(vu, kb)
    st_ref[...] = gst

    y = jnp.concatenate(ys, axis=0)
    inv_hd = 1.0 / hd
    mean = _dot01_right(y, head_ones) * inv_hd
    yc = y - mean
    var = _dot01_right(yc * yc, head_ones) * inv_hd
    yn = yc * lax.rsqrt(var + GN_EPS) * gng_ref[...] + gnb_ref[...]
    bonus = _dot01_right(r * kp * rk_ref[...], head_ones) * v
    o_ref[...] = (yn + bonus) * g_ref[...]


def _rwkv(r, k, v, lw, a, g, k_k, k_a, r_k, gn_g, gn_b, *, batch, seq):
    n, d = r.shape
    width = 2 * RWKV_HEAD
    rows = min(RWKV_ROWS, seq)
    steps = seq // rows
    blk = pl.BlockSpec((rows, width), lambda b, h, t: (b * steps + t, h))
    par = pl.BlockSpec((1, width), lambda b, h, t: (0, h))
    return pl.pallas_call(
        functools.partial(_rwkv_kernel, rows=rows),
        grid=(batch, d // width, steps),
        in_specs=[blk] * 6 + [par] * 5,
        out_specs=blk,
        out_shape=jax.ShapeDtypeStruct((n, d), F32),
        scratch_shapes=[pltpu.VMEM((width, width), F32)],
        compiler_params=_cparams("parallel", "parallel", "arbitrary"),
        name="rwkv7",
    )(r, k, v, lw, a, g, k_k, k_a, r_k, gn_g, gn_b)


def _proj_norm_kernel(x_ref, z_ref, w_ref, g_ref, b_ref, o_ref, *, alpha):
    y = _dot(z_ref[...].astype(BF16), w_ref[...])
    o_ref[...] = _layer_norm(alpha * x_ref[...] + y, g_ref[...], b_ref[...])


def _proj_norm(x, z, w, ln_g, ln_b, *, seq, alpha):
    n, d = x.shape
    tm = min(ROW_TILE, seq)
    return pl.pallas_call(
        functools.partial(_proj_norm_kernel, alpha=alpha),
        grid=(n // tm,),
        in_specs=[pl.BlockSpec((tm, d), lambda i: (i, 0)), pl.BlockSpec((tm, d), lambda i: (i, 0)),
                  _const_spec((d, d)), _const_spec((1, d)), _const_spec((1, d))],
        out_specs=pl.BlockSpec((tm, d), lambda i: (i, 0)),
        out_shape=jax.ShapeDtypeStruct((n, d), F32),
        compiler_params=_cparams("parallel"),
        name="rw_out_proj",
    )(x, z, w, ln_g, ln_b)


def _pad_cols(w, mult):
    pad = (-w.shape[1]) % mult
    return jnp.pad(w, ((0, 0), (0, pad)))


def _pad_rows(w, mult):
    pad = (-w.shape[0]) % mult
    return jnp.pad(w, ((0, pad), (0, 0)))


def kernel(x, ln_mix_g, ln_mix_b, ln_ffn_g, ln_ffn_b, ev_w_in, ev_b_in, ev_conv_w, ev_conv_b, ev_cln_g, ev_cln_b, ev_lb_logits, ev_onorm_g, ev_w_out, rw_mu, rw_w_r, rw_w_k, rw_w_v, rw_w_o, rw_w0, rw_w1, rw_w2, rw_a0, rw_a1, rw_a2, rw_g1, rw_g2, rw_k_k, rw_k_a, rw_r_k, rw_gn_g, rw_gn_b, ff_w_up, ff_w_gate, ff_conv_w, ff_conv_b, ff_w_down):
    batch, seq, d = x.shape
    depth = ln_mix_g.shape[0]
    alpha = (2 * depth) ** 0.25
    bf = lambda w: w.astype(BF16)
    row = lambda p: p.reshape(1, -1)
    h = x.reshape(batch * seq, d)
    for layer in range(depth):
        j = layer // 2
        if layer % 2 == 0:
            glu, q, lf, k, v, sg = _ev_in(h, bf(ev_w_in[j]), row(ev_b_in[j]), ev_lb_logits, layer_j=j, seq=seq)
            ob = _hgrn(q, lf, k, v, sg, row(ev_onorm_g[j]), batch=batch, seq=seq)
            h = _ev_out(h, glu, ob, ev_conv_w[j], row(ev_conv_b[j]), row(ev_cln_g[j]), row(ev_cln_b[j]),
                        bf(ev_w_out[j]), row(ln_mix_g[layer]), row(ln_mix_b[layer]), seq=seq, alpha=alpha)
        else:
            r, k, v, lw, a, g = _rw_in(
                h, rw_mu[j], bf(rw_w_r[j]), bf(rw_w_k[j]), bf(rw_w_v[j]), row(rw_w0[j]),
                bf(_pad_cols(rw_w1[j], LANES)), bf(_pad_rows(rw_w2[j], LANES)), row(rw_a0[j]),
                bf(_pad_cols(rw_a1[j], LANES)), bf(_pad_rows(rw_a2[j], LANES)),
                bf(_pad_cols(rw_g1[j], LANES)), bf(_pad_rows(rw_g2[j], LANES)), seq=seq)
            z = _rwkv(r, k, v, lw, a, g, row(rw_k_k[j]), row(rw_k_a[j]), row(rw_r_k[j]), row(rw_gn_g[j]),
                      row(rw_gn_b[j]), batch=batch, seq=seq)
            h = _proj_norm(h, z, bf(rw_w_o[j]), row(ln_mix_g[layer]), row(ln_mix_b[layer]), seq=seq, alpha=alpha)
        h = _ffn(h, bf(ff_w_up[layer]), bf(ff_w_gate[layer]), ff_conv_w[layer], row(ff_conv_b[layer]),
                 bf(ff_w_down[layer]), row(ln_ffn_g[layer]), row(ln_ffn_b[layer]), seq=seq, alpha=alpha)
    return h.reshape(batch, seq, d)
```

```python
import functools

import jax
import jax.numpy as jnp
from jax import lax
from jax.experimental import pallas as pl
from jax.experimental.pallas import tpu as pltpu

F32 = jnp.float32
BF16 = jnp.bfloat16

HGRN_HEAD = 128
RWKV_HEAD = 64
LN_EPS = 1e-5
GN_EPS = 64e-5
RMS_EPS = 1e-6
KK_EPS = 1e-12

LANES = 128
SUBLANES = 8
VMEM_LIMIT_BYTES = 56 * 1024 * 1024

ROW_TILE = 512
FF_CHUNK = 256
HGRN_SUB = 16
HGRN_ROWS = 128
RWKV_CHUNK = 64
RWKV_GROUP = 4
RWKV_ROWS = 1024
CONV_HALO = 32


def _cparams(*sem):
    return pltpu.CompilerParams(dimension_semantics=sem, vmem_limit_bytes=VMEM_LIMIT_BYTES)


def _dot(a, b):
    return jnp.dot(a, b, preferred_element_type=F32)


def _dot_nt(a, b):
    return lax.dot_general(a, b, (((1,), (1,)), ((), ())), preferred_element_type=F32)


def _dot_tn(a, b):
    return lax.dot_general(a, b, (((0,), (0,)), ((), ())), preferred_element_type=F32)


def _split3(x):
    hi = x.astype(BF16)
    r1 = x - hi.astype(F32)
    mid = r1.astype(BF16)
    lo = (r1 - mid.astype(F32)).astype(BF16)
    return hi, mid, lo


def _dot01_left(m01, x):
    hi, mid, lo = _split3(x)
    return _dot(m01, hi) + _dot(m01, mid) + _dot(m01, lo)


def _sigmoid(x):
    return 1.0 / (1.0 + jnp.exp(-x))


def _layer_norm(y, g, b):
    mu = jnp.mean(y, axis=-1, keepdims=True)
    yc = y - mu
    var = jnp.mean(yc * yc, axis=-1, keepdims=True)
    return yc * lax.rsqrt(var + LN_EPS) * g + b


def _shift_rows(u, prev, k):
    rows = u.shape[0]
    row = lax.broadcasted_iota(jnp.int32, (rows, 1), 0)
    out = pltpu.roll(u, k, 0)
    for j in range(k):
        out = jnp.where(row == j, prev[SUBLANES - k + j:SUBLANES - k + j + 1], out)
    return out


def _const_spec(shape):
    return pl.BlockSpec(shape, lambda *_: (0,) * len(shape))


def _halo_index(i, tm, halo):
    return jnp.maximum(i * (tm // halo) - 1, 0)


def _ffn_kernel(x_ref, xh_ref, wup_ref, wgate_ref, cw_ref, cb_ref, wdown_ref, g_ref, b_ref, o_ref, h_ref,
                *, tm, seq, alpha):
    i = pl.program_id(0)
    x = x_ref[...]
    xb = x.astype(BF16)
    xhb = xh_ref[...].astype(BF16)
    at_seq_start = (i * tm) % seq == 0
    dff = wup_ref.shape[1]
    for c in range(dff // FF_CHUNK):
        sl = slice(c * FF_CHUNK, (c + 1) * FF_CHUNK)
        wu = wup_ref[:, sl]
        u = _dot(xb, wu)
        uh = jnp.where(at_seq_start, 0.0, _dot(xhb, wu))
        cw = cw_ref[:, sl]
        conv = cw[2:3] * u + cw[1:2] * _shift_rows(u, uh, 1) + cw[0:1] * _shift_rows(u, uh, 2) + cb_ref[:, sl]
        gelu = 0.5 * conv * (1.0 + lax.erf(conv * (0.5 ** 0.5)))
        h_ref[:, sl] = (gelu * _dot(xb, wgate_ref[:, sl])).astype(BF16)
    f = _dot(h_ref[...], wdown_ref[...])
    o_ref[...] = _layer_norm(alpha * x + f, g_ref[...], b_ref[...])


def _ffn(x, w_up, w_gate, conv_w, conv_b, w_down, ln_g, ln_b, *, seq, alpha):
    n, d = x.shape
    dff = w_up.shape[1]
    tm = min(ROW_TILE, seq)
    return pl.pallas_call(
        functools.partial(_ffn_kernel, tm=tm, seq=seq, alpha=alpha),
        grid=(n // tm,),
        in_specs=[
            pl.BlockSpec((tm, d), lambda i: (i, 0)),
            pl.BlockSpec((SUBLANES, d), lambda i: (_halo_index(i, tm, SUBLANES), 0)),
            _const_spec((d, dff)), _const_spec((d, dff)), _const_spec(conv_w.shape), _const_spec((1, dff)),
            _const_spec((dff, d)), _const_spec((1, d)), _const_spec((1, d)),
        ],
        out_specs=pl.BlockSpec((tm, d), lambda i: (i, 0)),
        out_shape=jax.ShapeDtypeStruct((n, d), F32),
        scratch_shapes=[pltpu.VMEM((tm, dff), BF16)],
        compiler_params=_cparams("parallel"),
        name="conv_ffn",
    )(x, x, w_up, w_gate, conv_w, conv_b, w_down, ln_g, ln_b)


def _ev_in_kernel(x_ref, w_ref, b_ref, lbl_ref, glu_ref, q_ref, lf_ref, k_ref, v_ref, sg_ref, *, layer_j):
    xb = x_ref[...].astype(BF16)
    width = glu_ref.shape[1]

    def proj(c):
        sl = slice(c * width, (c + 1) * width)
        return _dot(xb, w_ref[:, sl]) + b_ref[:, sl]

    glu_ref[...] = proj(0) * _sigmoid(proj(1))
    q = proj(2)
    q_ref[...] = q * _sigmoid(q)

    logits = lbl_ref[...]
    e = jnp.exp(logits - jnp.max(logits, axis=0, keepdims=True))
    p = e / jnp.sum(e, axis=0, keepdims=True)
    lb = jnp.zeros_like(p[0:1])
    for r in range(1, layer_j + 1):
        lb = lb + p[r:r + 1]

    fz = proj(3)
    en = jnp.exp(-jnp.abs(fz))
    inv = 1.0 / (1.0 + en)
    log_sig = jnp.minimum(fz, 0.0) - jnp.log1p(en)
    sig_neg = jnp.where(fz >= 0.0, en * inv, inv)
    a = jnp.log(lb)
    bterm = jnp.log1p(-lb) + log_sig
    lf_ref[...] = jnp.maximum(a, bterm) + jnp.log1p(jnp.exp(-jnp.abs(a - bterm)))
    k_ref[...] = (1.0 - lb) * sig_neg
    v_ref[...] = proj(4)
    gate = proj(5)
    sg_ref[...] = gate * _sigmoid(gate)


def _ev_in(x, w_in, b_in, lb_logits, *, layer_j, seq):
    n, d = x.shape
    din = w_in.shape[1]
    width = din // 6
    tm = min(ROW_TILE, seq)
    out = jax.ShapeDtypeStruct((n, width), F32)
    return pl.pallas_call(
        functools.partial(_ev_in_kernel, layer_j=layer_j),
        grid=(n // tm,),
        in_specs=[pl.BlockSpec((tm, d), lambda i: (i, 0)), _const_spec((d, din)), _const_spec((1, din)),
                  _const_spec(lb_logits.shape)],
        out_specs=[pl.BlockSpec((tm, width), lambda i: (i, 0))] * 6,
        out_shape=[out] * 6,
        compiler_params=_cparams("parallel"),
        name="ev_in_proj",
    )(x, w_in, b_in, lb_logits)


def _hgrn_kernel(q_ref, lf_ref, k_ref, v_ref, sg_ref, g_ref, o_ref, st_ref, *, rows):
    @pl.when(pl.program_id(2) == 0)
    def _():
        st_ref[...] = jnp.zeros_like(st_ref)

    sub = HGRN_SUB
    r_i = lax.broadcasted_iota(jnp.int32, (rows, rows), 0)
    c_i = lax.broadcasted_iota(jnp.int32, (rows, rows), 1)
    same = (r_i // sub) == (c_i // sub)
    tri = jnp.where(same & (c_i <= r_i), 1.0, 0.0).astype(BF16)
    blk = jnp.where(same, 1.0, 0.0).astype(BF16)

    lf = lf_ref[...]
    lc = _dot01_left(tri, lf)
    tot = _dot01_left(blk, lf)
    q = q_ref[...]
    k = k_ref[...]
    v = v_ref[...]
    q_in = (q * jnp.exp(lc)).astype(BF16)
    k_out = (k * jnp.exp(tot - lc)).astype(BF16)
    dec = jnp.exp(tot)
    vb = v.astype(BF16)

    t_i = lax.broadcasted_iota(jnp.int32, (sub, 1), 0)
    st = st_ref[...]
    outs = []
    for i in range(rows // sub):
        rs = slice(i * sub, (i + 1) * sub)
        o = _dot_nt(q_in[rs], st.astype(BF16))
        lci, qi, ki, vi = lc[rs], q[rs], k[rs], v[rs]
        for s in range(sub):
            e = jnp.exp(jnp.minimum(lci - lci[s:s + 1], 0.0))
            col = jnp.sum(qi * (ki[s:s + 1] * e), axis=-1, keepdims=True)
            o = o + jnp.where(t_i >= s, col, 0.0) * vi[s:s + 1]
        outs.append(o)
        st = st * dec[i * sub:i * sub + 1] + _dot_tn(vb[rs], k_out[rs])
    st_ref[...] = st

    o = jnp.concatenate(outs, axis=0)
    o = o * lax.rsqrt(jnp.mean(o * o, axis=-1, keepdims=True) + RMS_EPS)
    o_ref[...] = o * g_ref[...] * sg_ref[...]


def _hgrn(q, lf, k, v, sg, onorm_g, *, batch, seq):
    n, width = q.shape
    heads = width // HGRN_HEAD
    rows = min(HGRN_ROWS, seq)
    steps = seq // rows
    blk = pl.BlockSpec((rows, HGRN_HEAD), lambda b, h, t: (b * steps + t, h))
    return pl.pallas_call(
        functools.partial(_hgrn_kernel, rows=rows),
        grid=(batch, heads, steps),
        in_specs=[blk] * 5 + [pl.BlockSpec((1, HGRN_HEAD), lambda b, h, t: (0, h))],
        out_specs=blk,
        out_shape=jax.ShapeDtypeStruct((n, width), F32),
        scratch_shapes=[pltpu.VMEM((HGRN_HEAD, HGRN_HEAD), F32)],
        compiler_params=_cparams("parallel", "parallel", "arbitrary"),
        name="hgrn2",
    )(q, lf, k, v, sg, onorm_g)


def _ev_out_kernel(x_ref, glu_ref, gh_ref, ob_ref, cw_ref, cb_ref, cg_ref, cbeta_ref, wout_ref, g_ref, b_ref,
                   o_ref, gs_ref, *, tm, seq, alpha):
    i = pl.program_id(0)
    at_seq_start = (i * tm) % seq == 0
    taps, width = cw_ref.shape
    gs_ref[0:CONV_HALO, :] = jnp.where(at_seq_start, 0.0, gh_ref[...])
    gs_ref[CONV_HALO:CONV_HALO + tm, :] = glu_ref[...]
    acc = jnp.zeros((tm, width), F32) + cb_ref[...]
    first = CONV_HALO - (taps - 1)
    for j in range(taps):
        acc = acc + cw_ref[j:j + 1, :] * gs_ref[first + j:first + j + tm, :]
    ua = _layer_norm(acc, cg_ref[...], cbeta_ref[...])
    ua = ua * _sigmoid(ua)
    y = _dot(ua.astype(BF16), wout_ref[0:width, :]) + _dot(ob_ref[...].astype(BF16), wout_ref[width:, :])
    o_ref[...] = _layer_norm(alpha * x_ref[...] + y, g_ref[...], b_ref[...])


def _ev_out(x, glu, ob, conv_w, conv_b, cln_g, cln_b, w_out, ln_g, ln_b, *, seq, alpha):
    n, d = x.shape
    width = glu.shape[1]
    tm = min(ROW_TILE, seq)
    return pl.pallas_call(
        functools.partial(_ev_out_kernel, tm=tm, seq=seq, alpha=alpha),
        grid=(n // tm,),
        in_specs=[
            pl.BlockSpec((tm, d), lambda i: (i, 0)),
            pl.BlockSpec((tm, width), lambda i: (i, 0)),
            pl.BlockSpec((CONV_HALO, width), lambda i: (_halo_index(i, tm, CONV_HALO), 0)),
            pl.BlockSpec((tm, width), lambda i: (i, 0)),
            _const_spec(conv_w.shape), _const_spec((1, width)), _const_spec((1, width)), _const_spec((1, width)),
            _const_spec((d, d)), _const_spec((1, d)), _const_spec((1, d)),
        ],
        out_specs=pl.BlockSpec((tm, d), lambda i: (i, 0)),
        out_shape=jax.ShapeDtypeStruct((n, d), F32),
        scratch_shapes=[pltpu.VMEM((CONV_HALO + tm, width), F32)],
        compiler_params=_cparams("parallel"),
        name="ev_out_proj",
    )(x, glu, glu, ob, conv_w, conv_b, cln_g, cln_b, w_out, ln_g, ln_b)


def _rw_in_kernel(x_ref, xh_ref, mu_ref, wr_ref, wk_ref, wv_ref, w0_ref, w1_ref, w2_ref, a0_ref, a1_ref, a2_ref,
                  g1_ref, g2_ref, r_ref, k_ref, v_ref, lw_ref, a_ref, g_ref, *, tm, seq):
    i = pl.program_id(0)
    x = x_ref[...]
    prev = jnp.where((i * tm) % seq == 0, 0.0, xh_ref[...])
    xx = _shift_rows(x, prev, 1) - x

    def mix(j):
        return (x + xx * mu_ref[j:j + 1, :]).astype(BF16)

    r_ref[...] = _dot(mix(0), wr_ref[...])
    z = w0_ref[...] + _dot(jnp.tanh(_dot(mix(1), w1_ref[...])).astype(BF16), w2_ref[...])
    lw_ref[...] = -(_sigmoid(z) * (jnp.exp(-0.5)))
    k_ref[...] = _dot(mix(2), wk_ref[...])
    v_ref[...] = _dot(mix(3), wv_ref[...])
    a_ref[...] = _sigmoid(a0_ref[...] + _dot(_dot(mix(4), a1_ref[...]).astype(BF16), a2_ref[...]))
    g_ref[...] = _dot(_sigmoid(_dot(mix(5), g1_ref[...])).astype(BF16), g2_ref[...])


def _rw_in(x, mu, w_r, w_k, w_v, w0, w1, w2, a0, a1, a2, g1, g2, *, seq):
    n, d = x.shape
    tm = min(ROW_TILE, seq)
    out = jax.ShapeDtypeStruct((n, d), F32)
    consts = (mu, w_r, w_k, w_v, w0, w1, w2, a0, a1, a2, g1, g2)
    return pl.pallas_call(
        functools.partial(_rw_in_kernel, tm=tm, seq=seq),
        grid=(n // tm,),
        in_specs=[pl.BlockSpec((tm, d), lambda i: (i, 0)),
                  pl.BlockSpec((SUBLANES, d), lambda i: (_halo_index(i, tm, SUBLANES), 0))]
                 + [_const_spec(c.shape) for c in consts],
        out_specs=[pl.BlockSpec((tm, d), lambda i: (i, 0))] * 6,
        out_shape=[out] * 6,
        compiler_params=_cparams("parallel"),
        name="rwkv_in_proj",
    )(x, x, *consts)


def _cumsum_rows(x):
    n = x.shape[0]
    row = lax.broadcasted_iota(jnp.int32, (n, 1), 0)
    s = 1
    while s < n:
        x = x + jnp.where(row >= s, pltpu.roll(x, s, 0), 0.0)
        s *= 2
    return x


def _group_sums(x, ones_bd):
    hi = x.astype(BF16)
    lo = (x - hi.astype(F32)).astype(BF16)
    return _dot(hi, ones_bd) + _dot(lo, ones_bd)


def _rwkv_kernel(r_ref, k_ref, v_ref, lw_ref, a_ref, g_ref, kk_ref, ka_ref, rk_ref, gng_ref, gnb_ref, o_ref,
                 st_ref, *, rows):
    @pl.when(pl.program_id(2) == 0)
    def _():
        st_ref[...] = jnp.zeros_like(st_ref)

    ch = RWKV_CHUNK
    hd = RWKV_HEAD
    assert ch == hd, "folded matrices and head operands share one block-diagonal mask"
    width = r_ref.shape[1]
    heads = width // hd
    r_i = lax.broadcasted_iota(jnp.int32, (width, width), 0)
    c_i = lax.broadcasted_iota(jnp.int32, (width, width), 1)
    same_head = (r_i // hd) == (c_i // hd)
    ones_bd = jnp.where(same_head, 1.0, 0.0).astype(BF16)
    t_i = lax.broadcasted_iota(jnp.int32, (ch, width), 0)
    s_i = lax.broadcasted_iota(jnp.int32, (ch, width), 1) % ch
    strict = t_i > s_i
    incl = t_i >= s_i
    eye = jnp.where(t_i == s_i, 1.0, 0.0)
    zero_b = jnp.zeros((width, width), BF16)

    def stack(x):
        xb = x.astype(BF16)
        return jnp.where(same_head, jnp.concatenate([xb] * heads, axis=0), zero_b)

    r = r_ref[...]
    k = k_ref[...]
    v = v_ref[...]
    lw = lw_ref[...]
    a = a_ref[...]

    kkr = k * kk_ref[...]
    kk = kkr / jnp.maximum(jnp.sqrt(_group_sums(kkr * kkr, ones_bd)), KK_EPS)
    kp = k * (1.0 + (a - 1.0) * ka_ref[...])
    beta = kk * a

    chunks = range(rows // ch)
    sl = [slice(c * ch, (c + 1) * ch) for c in chunks]
    r_t, a_t, kb, dec, p_k, p_b = [], [], [], [], [], []
    for c in chunks:
        lwc = lw[sl[c]]
        cum = _cumsum_rows(lwc)
        tot = cum[ch - 1:ch]
        e_neg = jnp.exp(-cum)
        e_out = jnp.exp(tot - cum)
        r_t.append(r[sl[c]] * jnp.exp(cum))
        a_t.append(-kk[sl[c]] * jnp.exp(cum - lwc))
        kb.append(jnp.concatenate([kp[sl[c]] * e_out, beta[sl[c]] * e_out], axis=0).astype(BF16))
        dec.append(jnp.exp(tot))
        ar = jnp.concatenate([a_t[c], r_t[c]], axis=0).astype(BF16)
        p_k.append(_dot_nt(ar, stack(kp[sl[c]] * e_neg)))
        p_b.append(_dot_nt(ar, stack(beta[sl[c]] * e_neg)))
    l_ab = [jnp.where(strict, p_b[c][0:ch], 0.0) for c in chunks]
    a_rb = [jnp.where(incl, p_b[c][ch:], 0.0).astype(BF16) for c in chunks]

    tinv = [eye + l_ab[c] for c in chunks]
    lp = [_dot(l_ab[c].astype(BF16), stack(l_ab[c])) for c in chunks]
    lv, arkv = [], []
    for c in chunks:
        lk = jnp.concatenate([jnp.where(strict, p_k[c][0:ch], 0.0), jnp.where(incl, p_k[c][ch:], 0.0)], axis=0)
        z = _dot(lk.astype(BF16), stack(v[sl[c]]))
        lv.append(z[0:ch])
        arkv.append(z[ch:])
    n_sq = ch.bit_length() - 2
    for s in range(1, n_sq + 1):
        for c in chunks:
            lp_bd = stack(lp[c])
            if s < n_sq:
                z = _dot(jnp.concatenate([tinv[c], lp[c]], axis=0).astype(BF16), lp_bd)
                tinv[c] = tinv[c] + z[0:ch]
                lp[c] = z[ch:]
            else:
                tinv[c] = tinv[c] + _dot(tinv[c].astype(BF16), lp_bd)
    ta, tlv_t, vk = [], [], []
    for c in chunks:
        tb = tinv[c].astype(BF16)
        ta.append(_dot(tb, stack(a_t[c])).astype(BF16))
        tlv_t.append(_dot(tb, stack(lv[c])).T)
        vk.append(jnp.where(same_head, _dot_tn(v[sl[c]].astype(BF16), kb[c][0:ch]), 0.0))

    gst = st_ref[...]
    g_in, u_t = [], []
    for c in chunks:
        gb = gst.astype(BF16)
        g_in.append(gb)
        ut = (_dot_nt(gb, ta[c]) + tlv_t[c]).astype(BF16)
        u_t.append(ut)
        gst = gst * dec[c] + vk[c] + jnp.where(same_head, _dot(ut, kb[c][ch:]), 0.0)
    st_ref[...] = gst

    ys = []
    for c in chunks:
        ut_bd = jnp.where(same_head, jnp.concatenate([u_t[c]] * heads, axis=1), zero_b)
        ys.append(_dot_nt(r_t[c].astype(BF16), g_in[c]) + arkv[c] + _dot_nt(a_rb[c], ut_bd))

    y = jnp.concatenate(ys, axis=0)
    inv_hd = 1.0 / hd
    yc = y - _group_sums(y, ones_bd) * inv_hd
    var = _group_sums(yc * yc, ones_bd) * inv_hd
    yn = yc * lax.rsqrt(var + GN_EPS) * gng_ref[...] + gnb_ref[...]
    bonus = _group_sums(r * kp * rk_ref[...], ones_bd) * v
    o_ref[...] = (yn + bonus) * g_ref[...]


def _rwkv(r, k, v, lw, a, g, k_k, k_a, r_k, gn_g, gn_b, *, batch, seq):
    n, d = r.shape
    width = RWKV_GROUP * RWKV_HEAD
    rows = min(RWKV_ROWS, seq)
    steps = seq // rows
    blk = pl.BlockSpec((rows, width), lambda b, h, t: (b * steps + t, h))
    par = pl.BlockSpec((1, width), lambda b, h, t: (0, h))
    return pl.pallas_call(
        functools.partial(_rwkv_kernel, rows=rows),
        grid=(batch, d // width, steps),
        in_specs=[blk] * 6 + [par] * 5,
        out_specs=blk,
        out_shape=jax.ShapeDtypeStruct((n, d), F32),
        scratch_shapes=[pltpu.VMEM((width, width), F32)],
        compiler_params=_cparams("parallel", "parallel", "arbitrary"),
        name="rwkv7",
    )(r, k, v, lw, a, g, k_k, k_a, r_k, gn_g, gn_b)


def _proj_norm_kernel(x_ref, z_ref, w_ref, g_ref, b_ref, o_ref, *, alpha):
    y = _dot(z_ref[...].astype(BF16), w_ref[...])
    o_ref[...] = _layer_norm(alpha * x_ref[...] + y, g_ref[...], b_ref[...])


def _proj_norm(x, z, w, ln_g, ln_b, *, seq, alpha):
    n, d = x.shape
    tm = min(ROW_TILE, seq)
    return pl.pallas_call(
        functools.partial(_proj_norm_kernel, alpha=alpha),
        grid=(n // tm,),
        in_specs=[pl.BlockSpec((tm, d), lambda i: (i, 0)), pl.BlockSpec((tm, d), lambda i: (i, 0)),
                  _const_spec((d, d)), _const_spec((1, d)), _const_spec((1, d))],
        out_specs=pl.BlockSpec((tm, d), lambda i: (i, 0)),
        out_shape=jax.ShapeDtypeStruct((n, d), F32),
        compiler_params=_cparams("parallel"),
        name="rw_out_proj",
    )(x, z, w, ln_g, ln_b)


def _pad_cols(w, mult):
    pad = (-w.shape[1]) % mult
    return jnp.pad(w, ((0, 0), (0, pad)))


def _pad_rows(w, mult):
    pad = (-w.shape[0]) % mult
    return jnp.pad(w, ((0, pad), (0, 0)))


def kernel(x, ln_mix_g, ln_mix_b, ln_ffn_g, ln_ffn_b, ev_w_in, ev_b_in, ev_conv_w, ev_conv_b, ev_cln_g, ev_cln_b, ev_lb_logits, ev_onorm_g, ev_w_out, rw_mu, rw_w_r, rw_w_k, rw_w_v, rw_w_o, rw_w0, rw_w1, rw_w2, rw_a0, rw_a1, rw_a2, rw_g1, rw_g2, rw_k_k, rw_k_a, rw_r_k, rw_gn_g, rw_gn_b, ff_w_up, ff_w_gate, ff_conv_w, ff_conv_b, ff_w_down):
    batch, seq, d = x.shape
    depth = ln_mix_g.shape[0]
    alpha = (2 * depth) ** 0.25
    bf = lambda w: w.astype(BF16)
    row = lambda p: p.reshape(1, -1)
    h = x.reshape(batch * seq, d)
    for layer in range(depth):
        j = layer // 2
        if layer % 2 == 0:
            glu, q, lf, k, v, sg = _ev_in(h, bf(ev_w_in[j]), row(ev_b_in[j]), ev_lb_logits, layer_j=j, seq=seq)
            ob = _hgrn(q, lf, k, v, sg, row(ev_onorm_g[j]), batch=batch, seq=seq)
            h = _ev_out(h, glu, ob, ev_conv_w[j], row(ev_conv_b[j]), row(ev_cln_g[j]), row(ev_cln_b[j]),
                        bf(ev_w_out[j]), row(ln_mix_g[layer]), row(ln_mix_b[layer]), seq=seq, alpha=alpha)
        else:
            r, k, v, lw, a, g = _rw_in(
                h, rw_mu[j], bf(rw_w_r[j]), bf(rw_w_k[j]), bf(rw_w_v[j]), row(rw_w0[j]),
                bf(_pad_cols(rw_w1[j], LANES)), bf(_pad_rows(rw_w2[j], LANES)), row(rw_a0[j]),
                bf(_pad_cols(rw_a1[j], LANES)), bf(_pad_rows(rw_a2[j], LANES)),
                bf(_pad_cols(rw_g1[j], LANES)), bf(_pad_rows(rw_g2[j], LANES)), seq=seq)
            z = _rwkv(r, k, v, lw, a, g, row(rw_k_k[j]), row(rw_k_a[j]), row(rw_r_k[j]), row(rw_gn_g[j]),
                      row(rw_gn_b[j]), batch=batch, seq=seq)
            h = _proj_norm(h, z, bf(rw_w_o[j]), row(ln_mix_g[layer]), row(ln_mix_b[layer]), seq=seq, alpha=alpha)
        h = _ffn(h, bf(ff_w_up[layer]), bf(ff_w_gate[layer]), ff_conv_w[layer], row(ff_conv_b[layer]),
                 bf(ff_w_down[layer]), row(ln_ffn_g[layer]), row(ln_ffn_b[layer]), seq=seq, alpha=alpha)
    return h.reshape(batch, seq, d)
```

```python
import functools

import jax
import jax.numpy as jnp
from jax import lax
from jax.experimental import pallas as pl
from jax.experimental.pallas import tpu as pltpu

F32 = jnp.float32
BF16 = jnp.bfloat16

HGRN_HEAD = 128
RWKV_HEAD = 64
LN_EPS = 1e-5
GN_EPS = 64e-5
RMS_EPS = 1e-6
KK_EPS = 1e-12

LANES = 128
SUBLANES = 8
VMEM_LIMIT_BYTES = 56 * 1024 * 1024

ROW_TILE = 512
FF_CHUNK = 256
HGRN_ROWS = 256
RWKV_CHUNK = 64
RWKV_GROUP = 4
RWKV_ROWS = 1024
RWKV_LOOKAHEAD = 4
CONV_HALO = 32


def _cparams(*sem):
    return pltpu.CompilerParams(dimension_semantics=sem, vmem_limit_bytes=VMEM_LIMIT_BYTES)


def _dot(a, b):
    return jnp.dot(a, b, preferred_element_type=F32)


def _dot_nt(a, b):
    return lax.dot_general(a, b, (((1,), (1,)), ((), ())), preferred_element_type=F32)


def _dot_tn(a, b):
    return lax.dot_general(a, b, (((0,), (0,)), ((), ())), preferred_element_type=F32)


def _sigmoid(x):
    return 1.0 / (1.0 + jnp.exp(-x))


def _layer_norm(y, g, b):
    mu = jnp.mean(y, axis=-1, keepdims=True)
    yc = y - mu
    var = jnp.mean(yc * yc, axis=-1, keepdims=True)
    return yc * lax.rsqrt(var + LN_EPS) * g + b


def _cumsum_rows(x):
    n = x.shape[0]
    row = lax.broadcasted_iota(jnp.int32, (n, 1), 0)
    s = 1
    while s < n:
        x = x + jnp.where(row >= s, pltpu.roll(x, s, 0), 0.0)
        s *= 2
    return x


def _shift_rows(u, prev, k):
    rows = u.shape[0]
    row = lax.broadcasted_iota(jnp.int32, (rows, 1), 0)
    out = pltpu.roll(u, k, 0)
    for j in range(k):
        out = jnp.where(row == j, prev[SUBLANES - k + j:SUBLANES - k + j + 1], out)
    return out


def _const_spec(shape):
    return pl.BlockSpec(shape, lambda *_: (0,) * len(shape))


def _halo_index(i, tm, halo):
    return jnp.maximum(i * (tm // halo) - 1, 0)


def _ffn_kernel(x_ref, xh_ref, wup_ref, wgate_ref, cw_ref, cb_ref, wdown_ref, g_ref, b_ref, o_ref, h_ref,
                *, tm, seq, alpha):
    i = pl.program_id(0)
    x = x_ref[...]
    xb = x.astype(BF16)
    xhb = xh_ref[...].astype(BF16)
    at_seq_start = (i * tm) % seq == 0
    dff = wup_ref.shape[1]
    for c in range(dff // FF_CHUNK):
        sl = slice(c * FF_CHUNK, (c + 1) * FF_CHUNK)
        wu = wup_ref[:, sl]
        u = _dot(xb, wu)
        uh = jnp.where(at_seq_start, 0.0, _dot(xhb, wu))
        cw = cw_ref[:, sl]
        conv = cw[2:3] * u + cw[1:2] * _shift_rows(u, uh, 1) + cw[0:1] * _shift_rows(u, uh, 2) + cb_ref[:, sl]
        gelu = 0.5 * conv * (1.0 + lax.erf(conv * (0.5 ** 0.5)))
        h_ref[:, sl] = (gelu * _dot(xb, wgate_ref[:, sl])).astype(BF16)
    f = _dot(h_ref[...], wdown_ref[...])
    o_ref[...] = _layer_norm(alpha * x + f, g_ref[...], b_ref[...])


def _ffn(x, w_up, w_gate, conv_w, conv_b, w_down, ln_g, ln_b, *, seq, alpha):
    n, d = x.shape
    dff = w_up.shape[1]
    tm = min(ROW_TILE, seq)
    return pl.pallas_call(
        functools.partial(_ffn_kernel, tm=tm, seq=seq, alpha=alpha),
        grid=(n // tm,),
        in_specs=[
            pl.BlockSpec((tm, d), lambda i: (i, 0)),
            pl.BlockSpec((SUBLANES, d), lambda i: (_halo_index(i, tm, SUBLANES), 0)),
            _const_spec((d, dff)), _const_spec((d, dff)), _const_spec(conv_w.shape), _const_spec((1, dff)),
            _const_spec((dff, d)), _const_spec((1, d)), _const_spec((1, d)),
        ],
        out_specs=pl.BlockSpec((tm, d), lambda i: (i, 0)),
        out_shape=jax.ShapeDtypeStruct((n, d), F32),
        scratch_shapes=[pltpu.VMEM((tm, dff), BF16)],
        compiler_params=_cparams("parallel"),
        name="conv_ffn",
    )(x, x, w_up, w_gate, conv_w, conv_b, w_down, ln_g, ln_b)


def _ev_in_kernel(x_ref, w_ref, b_ref, lbl_ref, glu_ref, q_ref, lf_ref, k_ref, v_ref, sg_ref, *, layer_j):
    xb = x_ref[...].astype(BF16)
    width = glu_ref.shape[1]

    def proj(c):
        sl = slice(c * width, (c + 1) * width)
        return _dot(xb, w_ref[:, sl]) + b_ref[:, sl]

    glu_ref[...] = proj(0) * _sigmoid(proj(1))
    q = proj(2)
    q_ref[...] = q * _sigmoid(q)

    logits = lbl_ref[...]
    e = jnp.exp(logits - jnp.max(logits, axis=0, keepdims=True))
    p = e / jnp.sum(e, axis=0, keepdims=True)
    lb = jnp.zeros_like(p[0:1])
    for r in range(1, layer_j + 1):
        lb = lb + p[r:r + 1]

    fz = proj(3)
    en = jnp.exp(-jnp.abs(fz))
    inv = 1.0 / (1.0 + en)
    log_sig = jnp.minimum(fz, 0.0) - jnp.log1p(en)
    sig_neg = jnp.where(fz >= 0.0, en * inv, inv)
    a = jnp.log(lb)
    bterm = jnp.log1p(-lb) + log_sig
    lf_ref[...] = jnp.maximum(a, bterm) + jnp.log1p(jnp.exp(-jnp.abs(a - bterm)))
    k_ref[...] = (1.0 - lb) * sig_neg
    v_ref[...] = proj(4)
    gate = proj(5)
    sg_ref[...] = gate * _sigmoid(gate)


def _ev_in(x, w_in, b_in, lb_logits, *, layer_j, seq):
    n, d = x.shape
    din = w_in.shape[1]
    width = din // 6
    tm = min(ROW_TILE, seq)
    out = jax.ShapeDtypeStruct((n, width), F32)
    return pl.pallas_call(
        functools.partial(_ev_in_kernel, layer_j=layer_j),
        grid=(n // tm,),
        in_specs=[pl.BlockSpec((tm, d), lambda i: (i, 0)), _const_spec((d, din)), _const_spec((1, din)),
                  _const_spec(lb_logits.shape)],
        out_specs=[pl.BlockSpec((tm, width), lambda i: (i, 0))] * 6,
        out_shape=[out] * 6,
        compiler_params=_cparams("parallel"),
        name="ev_in_proj",
    )(x, w_in, b_in, lb_logits)


def _hgrn_kernel(q_ref, lf_ref, k_ref, v_ref, sg_ref, g_ref, o_ref, st_ref, *, rows):
    @pl.when(pl.program_id(2) == 0)
    def _():
        st_ref[...] = jnp.zeros_like(st_ref)

    lanes = q_ref.shape[1]
    lf = lf_ref[...]
    q = q_ref[...]
    k = k_ref[...]
    v = v_ref[...]
    cum = _cumsum_rows(lf)
    row = lax.broadcasted_iota(jnp.int32, (rows, 1), 0)
    t_i = lax.broadcasted_iota(jnp.int32, (rows, rows), 0)
    s_i = lax.broadcasted_iota(jnp.int32, (rows, rows), 1)
    level = jnp.where(s_i > t_i, -2, 31 - lax.clz(t_i ^ s_i))

    scores = jnp.where(level == -1, _dot_nt(q.astype(BF16), k.astype(BF16)), 0.0)
    b = rows // 2
    while b >= 1:
        if 2 * b >= SUBLANES:
            ends = [jnp.broadcast_to(cum[(2 * j + 1) * b - 1:(2 * j + 1) * b], (2 * b, lanes))
                    for j in range(rows // (2 * b))]
            expo = -jnp.abs(cum - (jnp.concatenate(ends, axis=0) if len(ends) > 1 else ends[0]))
        elif b == 2:
            m = row % 4
            expo = jnp.where(m == 0, pltpu.roll(lf, rows - 1, 0),
                             jnp.where(m == 1, 0.0, jnp.where(m == 2, lf, lf + pltpu.roll(lf, 1, 0))))
        else:
            expo = jnp.where(row % 2 == 1, lf, 0.0)
        e = jnp.exp(expo)
        s_b = _dot_nt((q * e).astype(BF16), (k * e).astype(BF16))
        scores = jnp.where(level == b.bit_length() - 1, s_b, scores)
        b //= 2

    st = st_ref[...]
    tot = cum[rows - 1:rows]
    o = (_dot(scores.astype(BF16), v.astype(BF16))
         + _dot_nt((q * jnp.exp(cum)).astype(BF16), st.astype(BF16)))
    st_ref[...] = st * jnp.exp(tot) + _dot_tn(v.astype(BF16), (k * jnp.exp(tot - cum)).astype(BF16))

    o = o * lax.rsqrt(jnp.mean(o * o, axis=-1, keepdims=True) + RMS_EPS)
    o_ref[...] = o * g_ref[...] * sg_ref[...]


def _hgrn(q, lf, k, v, sg, onorm_g, *, batch, seq):
    n, width = q.shape
    heads = width // HGRN_HEAD
    rows = min(HGRN_ROWS, seq)
    steps = seq // rows
    blk = pl.BlockSpec((rows, HGRN_HEAD), lambda b, h, t: (b * steps + t, h))
    return pl.pallas_call(
        functools.partial(_hgrn_kernel, rows=rows),
        grid=(batch, heads, steps),
        in_specs=[blk] * 5 + [pl.BlockSpec((1, HGRN_HEAD), lambda b, h, t: (0, h))],
        out_specs=blk,
        out_shape=jax.ShapeDtypeStruct((n, width), F32),
        scratch_shapes=[pltpu.VMEM((HGRN_HEAD, HGRN_HEAD), F32)],
        compiler_params=_cparams("parallel", "parallel", "arbitrary"),
        name="hgrn2",
    )(q, lf, k, v, sg, onorm_g)


def _ev_out_kernel(x_ref, glu_ref, gh_ref, ob_ref, cw_ref, cb_ref, cg_ref, cbeta_ref, wout_ref, g_ref, b_ref,
                   o_ref, *, tm, seq, alpha):
    i = pl.program_id(0)
    at_seq_start = (i * tm) % seq == 0
    taps, width = cw_ref.shape
    g = jnp.concatenate([jnp.where(at_seq_start, 0.0, gh_ref[...]), glu_ref[...]], axis=0)
    span = CONV_HALO + tm
    first = CONV_HALO - (taps - 1)
    acc = jnp.zeros((tm, width), F32) + cb_ref[...]
    for r in range(SUBLANES):
        gr = g if r == 0 else pltpu.roll(g, span - r, 0)
        for base in range(0, CONV_HALO + 1, SUBLANES):
            j = base + r - first
            if 0 <= j < taps:
                acc = acc + cw_ref[j:j + 1, :] * gr[base:base + tm]
    ua = _layer_norm(acc, cg_ref[...], cbeta_ref[...])
    ua = ua * _sigmoid(ua)
    y = _dot(ua.astype(BF16), wout_ref[0:width, :]) + _dot(ob_ref[...].astype(BF16), wout_ref[width:, :])
    o_ref[...] = _layer_norm(alpha * x_ref[...] + y, g_ref[...], b_ref[...])


def _ev_out(x, glu, ob, conv_w, conv_b, cln_g, cln_b, w_out, ln_g, ln_b, *, seq, alpha):
    n, d = x.shape
    width = glu.shape[1]
    tm = min(ROW_TILE, seq)
    return pl.pallas_call(
        functools.partial(_ev_out_kernel, tm=tm, seq=seq, alpha=alpha),
        grid=(n // tm,),
        in_specs=[
            pl.BlockSpec((tm, d), lambda i: (i, 0)),
            pl.BlockSpec((tm, width), lambda i: (i, 0)),
            pl.BlockSpec((CONV_HALO, width), lambda i: (_halo_index(i, tm, CONV_HALO), 0)),
            pl.BlockSpec((tm, width), lambda i: (i, 0)),
            _const_spec(conv_w.shape), _const_spec((1, width)), _const_spec((1, width)), _const_spec((1, width)),
            _const_spec((d, d)), _const_spec((1, d)), _const_spec((1, d)),
        ],
        out_specs=pl.BlockSpec((tm, d), lambda i: (i, 0)),
        out_shape=jax.ShapeDtypeStruct((n, d), F32),
        compiler_params=_cparams("parallel"),
        name="ev_out_proj",
    )(x, glu, glu, ob, conv_w, conv_b, cln_g, cln_b, w_out, ln_g, ln_b)


def _rw_in_kernel(x_ref, xh_ref, mu_ref, wr_ref, wk_ref, wv_ref, w0_ref, w1_ref, w2_ref, a0_ref, a1_ref, a2_ref,
                  g1_ref, g2_ref, r_ref, k_ref, v_ref, lw_ref, a_ref, g_ref, *, tm, seq):
    i = pl.program_id(0)
    x = x_ref[...]
    prev = jnp.where((i * tm) % seq == 0, 0.0, xh_ref[...])
    xx = _shift_rows(x, prev, 1) - x

    def mix(j):
        return (x + xx * mu_ref[j:j + 1, :]).astype(BF16)

    r_ref[...] = _dot(mix(0), wr_ref[...])
    z = w0_ref[...] + _dot(jnp.tanh(_dot(mix(1), w1_ref[...])).astype(BF16), w2_ref[...])
    lw_ref[...] = -(_sigmoid(z) * (jnp.exp(-0.5)))
    k_ref[...] = _dot(mix(2), wk_ref[...])
    v_ref[...] = _dot(mix(3), wv_ref[...])
    a_ref[...] = _sigmoid(a0_ref[...] + _dot(_dot(mix(4), a1_ref[...]).astype(BF16), a2_ref[...]))
    g_ref[...] = _dot(_sigmoid(_dot(mix(5), g1_ref[...])).astype(BF16), g2_ref[...])


def _rw_in(x, mu, w_r, w_k, w_v, w0, w1, w2, a0, a1, a2, g1, g2, *, seq):
    n, d = x.shape
    tm = min(ROW_TILE, seq)
    out = jax.ShapeDtypeStruct((n, d), F32)
    consts = (mu, w_r, w_k, w_v, w0, w1, w2, a0, a1, a2, g1, g2)
    return pl.pallas_call(
        functools.partial(_rw_in_kernel, tm=tm, seq=seq),
        grid=(n // tm,),
        in_specs=[pl.BlockSpec((tm, d), lambda i: (i, 0)),
                  pl.BlockSpec((SUBLANES, d), lambda i: (_halo_index(i, tm, SUBLANES), 0))]
                 + [_const_spec(c.shape) for c in consts],
        out_specs=[pl.BlockSpec((tm, d), lambda i: (i, 0))] * 6,
        out_shape=[out] * 6,
        compiler_params=_cparams("parallel"),
        name="rwkv_in_proj",
    )(x, x, *consts)


def _group_sums(x, ones_bd):
    return _dot(x.astype(BF16), ones_bd)


def _rwkv_kernel(r_ref, k_ref, v_ref, lw_ref, a_ref, g_ref, kk_ref, ka_ref, rk_ref, gng_ref, gnb_ref, o_ref,
                 st_ref, *, rows):
    @pl.when(pl.program_id(2) == 0)
    def _():
        st_ref[...] = jnp.zeros_like(st_ref)

    ch = RWKV_CHUNK
    hd = RWKV_HEAD
    assert ch == hd, "folded matrices and head operands share one block-diagonal mask"
    width = r_ref.shape[1]
    heads = width // hd
    r_i = lax.broadcasted_iota(jnp.int32, (width, width), 0)
    c_i = lax.broadcasted_iota(jnp.int32, (width, width), 1)
    same_head = (r_i // hd) == (c_i // hd)
    ones_bd = jnp.where(same_head, 1.0, 0.0).astype(BF16)
    t_i = lax.broadcasted_iota(jnp.int32, (ch, width), 0)
    s_i = lax.broadcasted_iota(jnp.int32, (ch, width), 1) % ch
    strict = t_i > s_i
    incl = t_i >= s_i
    eye = jnp.where(t_i == s_i, 1.0, 0.0)
    zero_b = jnp.zeros((width, width), BF16)

    def stack(x):
        xb = x.astype(BF16)
        return jnp.where(same_head, jnp.concatenate([xb] * heads, axis=0), zero_b)

    r = r_ref[...]
    k = k_ref[...]
    v = v_ref[...]
    lw = lw_ref[...]
    a = a_ref[...]

    kkr = k * kk_ref[...]
    kk = kkr / jnp.maximum(jnp.sqrt(_group_sums(kkr * kkr, ones_bd)), KK_EPS)
    kp = k * (1.0 + (a - 1.0) * ka_ref[...])
    beta = kk * a

    n_chunks = rows // ch
    sl = [slice(c * ch, (c + 1) * ch) for c in range(n_chunks)]
    r_t, a_t, kb, dec, a_rb, arkv, ta, vt = ({} for _ in range(8))
    g_in, vu_t, ys = {}, {}, {}

    def prepare(cs):
        p_k, p_b, l_ab, tinv, lp, lv = ({} for _ in range(6))
        for c in cs:
            lwc = lw[sl[c]]
            cum = _cumsum_rows(lwc)
            tot = cum[ch - 1:ch]
            e_neg = jnp.exp(-cum)
            e_out = jnp.exp(tot - cum)
            r_t[c] = r[sl[c]] * jnp.exp(cum)
            a_t[c] = -kk[sl[c]] * jnp.exp(cum - lwc)
            kb[c] = jnp.concatenate([kp[sl[c]] * e_out, beta[sl[c]] * e_out], axis=0).astype(BF16)
            dec[c] = jnp.exp(tot)
            ar = jnp.concatenate([a_t[c], r_t[c]], axis=0).astype(BF16)
            p_k[c] = _dot_nt(ar, stack(kp[sl[c]] * e_neg))
            p_b[c] = _dot_nt(ar, stack(beta[sl[c]] * e_neg))
        yield
        for c in cs:
            l_ab[c] = jnp.where(strict, p_b[c][0:ch], 0.0)
            a_rb[c] = jnp.where(incl, p_b[c][ch:], 0.0).astype(BF16)
            tinv[c] = eye + l_ab[c]
            lp[c] = _dot(l_ab[c].astype(BF16), stack(l_ab[c]))
        yield
        for c in cs:
            lk = jnp.concatenate([jnp.where(strict, p_k[c][0:ch], 0.0), jnp.where(incl, p_k[c][ch:], 0.0)], axis=0)
            z = _dot(lk.astype(BF16), stack(v[sl[c]]))
            lv[c] = z[0:ch]
            arkv[c] = z[ch:]
        yield
        n_sq = ch.bit_length() - 2
        for s in range(1, n_sq + 1):
            for c in cs:
                lp_bd = stack(lp[c])
                if s < n_sq:
                    z = _dot(jnp.concatenate([tinv[c], lp[c]], axis=0).astype(BF16), lp_bd)
                    tinv[c] = tinv[c] + z[0:ch]
                    lp[c] = z[ch:]
                else:
                    tinv[c] = tinv[c] + _dot(tinv[c].astype(BF16), lp_bd)
            yield
        for c in cs:
            tb = tinv[c].astype(BF16)
            t_a = _dot(tb, stack(a_t[c]))
            tlv = _dot(tb, stack(lv[c]))
            ta[c] = jnp.concatenate([jnp.zeros_like(t_a), t_a], axis=0).astype(BF16)
            vt[c] = jnp.concatenate([v[sl[c]], tlv], axis=0).T
        yield

    state = [st_ref[...]]

    def advance(cs):
        for c in cs:
            g_in[c] = state[0].astype(BF16)
            vu_t[c] = _dot_nt(g_in[c], ta[c]) + vt[c]
            yield
            state[0] = state[0] * dec[c] + jnp.where(same_head, _dot(vu_t[c].astype(BF16), kb[c]), 0.0)
            yield

    def finish(cs):
        for c in cs:
            y_g = _dot_nt(r_t[c].astype(BF16), g_in[c])
            yield
            u_t = vu_t[c][:, ch:].astype(BF16)
            ut_bd = jnp.where(same_head, jnp.concatenate([u_t] * heads, axis=1), zero_b)
            ys[c] = y_g + arkv[c] + _dot_nt(a_rb[c], ut_bd)
            yield

    groups = [list(range(i, min(i + RWKV_LOOKAHEAD, n_chunks))) for i in range(0, n_chunks, RWKV_LOOKAHEAD)]
    for _ in prepare(groups[0]):
        pass
    for gi, grp in enumerate(groups):
        nxt = prepare(groups[gi + 1]) if gi + 1 < len(groups) else iter(())
        prv = finish(groups[gi - 1]) if gi > 0 else iter(())
        for _ in advance(grp):
            next(nxt, None)
            next(prv, None)
        for _ in nxt:
            pass
        for _ in prv:
            pass
    for _ in finish(groups[-1]):
        pass
    st_ref[...] = state[0]

    y = jnp.concatenate([ys[c] for c in range(n_chunks)], axis=0)
    inv_hd = 1.0 / hd
    yc = y - _group_sums(y, ones_bd) * inv_hd
    var = _group_sums(yc * yc, ones_bd) * inv_hd
    yn = yc * lax.rsqrt(var + GN_EPS) * gng_ref[...] + gnb_ref[...]
    bonus = _group_sums(r * kp * rk_ref[...], ones_bd) * v
    o_ref[...] = (yn + bonus) * g_ref[...]


def _rwkv(r, k, v, lw, a, g, k_k, k_a, r_k, gn_g, gn_b, *, batch, seq):
    n, d = r.shape
    width = RWKV_GROUP * RWKV_HEAD
    rows = min(RWKV_ROWS, seq)
    steps = seq // rows
    blk = pl.BlockSpec((rows, width), lambda b, h, t: (b * steps + t, h))
    par = pl.BlockSpec((1, width), lambda b, h, t: (0, h))
    return pl.pallas_call(
        functools.partial(_rwkv_kernel, rows=rows),
        grid=(batch, d // width, steps),
        in_specs=[blk] * 6 + [par] * 5,
        out_specs=blk,
        out_shape=jax.ShapeDtypeStruct((n, d), F32),
        scratch_shapes=[pltpu.VMEM((width, width), F32)],
        compiler_params=_cparams("parallel", "parallel", "arbitrary"),
        name="rwkv7",
    )(r, k, v, lw, a, g, k_k, k_a, r_k, gn_g, gn_b)


def _proj_norm_kernel(x_ref, z_ref, w_ref, g_ref, b_ref, o_ref, *, alpha):
    y = _dot(z_ref[...].astype(BF16), w_ref[...])
    o_ref[...] = _layer_norm(alpha * x_ref[...] + y, g_ref[...], b_ref[...])


def _proj_norm(x, z, w, ln_g, ln_b, *, seq, alpha):
    n, d = x.shape
    tm = min(ROW_TILE, seq)
    return pl.pallas_call(
        functools.partial(_proj_norm_kernel, alpha=alpha),
        grid=(n // tm,),
        in_specs=[pl.BlockSpec((tm, d), lambda i: (i, 0)), pl.BlockSpec((tm, d), lambda i: (i, 0)),
                  _const_spec((d, d)), _const_spec((1, d)), _const_spec((1, d))],
        out_specs=pl.BlockSpec((tm, d), lambda i: (i, 0)),
        out_shape=jax.ShapeDtypeStruct((n, d), F32),
        compiler_params=_cparams("parallel"),
        name="rw_out_proj",
    )(x, z, w, ln_g, ln_b)


def _pad_cols(w, mult):
    pad = (-w.shape[1]) % mult
    return jnp.pad(w, ((0, 0), (0, pad)))


def _pad_rows(w, mult):
    pad = (-w.shape[0]) % mult
    return jnp.pad(w, ((0, pad), (0, 0)))


def kernel(x, ln_mix_g, ln_mix_b, ln_ffn_g, ln_ffn_b, ev_w_in, ev_b_in, ev_conv_w, ev_conv_b, ev_cln_g, ev_cln_b, ev_lb_logits, ev_onorm_g, ev_w_out, rw_mu, rw_w_r, rw_w_k, rw_w_v, rw_w_o, rw_w0, rw_w1, rw_w2, rw_a0, rw_a1, rw_a2, rw_g1, rw_g2, rw_k_k, rw_k_a, rw_r_k, rw_gn_g, rw_gn_b, ff_w_up, ff_w_gate, ff_conv_w, ff_conv_b, ff_w_down):
    batch, seq, d = x.shape
    depth = ln_mix_g.shape[0]
    alpha = (2 * depth) ** 0.25
    bf = lambda w: w.astype(BF16)
    row = lambda p: p.reshape(1, -1)
    h = x.reshape(batch * seq, d)
    for layer in range(depth):
        j = layer // 2
        if layer % 2 == 0:
            glu, q, lf, k, v, sg = _ev_in(h, bf(ev_w_in[j]), row(ev_b_in[j]), ev_lb_logits, layer_j=j, seq=seq)
            ob = _hgrn(q, lf, k, v, sg, row(ev_onorm_g[j]), batch=batch, seq=seq)
            h = _ev_out(h, glu, ob, ev_conv_w[j], row(ev_conv_b[j]), row(ev_cln_g[j]), row(ev_cln_b[j]),
                        bf(ev_w_out[j]), row(ln_mix_g[layer]), row(ln_mix_b[layer]), seq=seq, alpha=alpha)
        else:
            r, k, v, lw, a, g = _rw_in(
                h, rw_mu[j], bf(rw_w_r[j]), bf(rw_w_k[j]), bf(rw_w_v[j]), row(rw_w0[j]),
                bf(_pad_cols(rw_w1[j], LANES)), bf(_pad_rows(rw_w2[j], LANES)), row(rw_a0[j]),
                bf(_pad_cols(rw_a1[j], LANES)), bf(_pad_rows(rw_a2[j], LANES)),
                bf(_pad_cols(rw_g1[j], LANES)), bf(_pad_rows(rw_g2[j], LANES)), seq=seq)
            z = _rwkv(r, k, v, lw, a, g, row(rw_k_k[j]), row(rw_k_a[j]), row(rw_r_k[j]), row(rw_gn_g[j]),
                      row(rw_gn_b[j]), batch=batch, seq=seq)
            h = _proj_norm(h, z, bf(rw_w_o[j]), row(ln_mix_g[layer]), row(ln_mix_b[layer]), seq=seq, alpha=alpha)
        h = _ffn(h, bf(ff_w_up[layer]), bf(ff_w_gate[layer]), ff_conv_w[layer], row(ff_conv_b[layer]),
                 bf(ff_w_down[layer]), row(ln_ffn_g[layer]), row(ln_ffn_b[layer]), seq=seq, alpha=alpha)
    return h.reshape(batch, seq, d)
```

```python
import functools

import jax
import jax.numpy as jnp
from jax import lax
from jax.experimental import pallas as pl
from jax.experimental.pallas import tpu as pltpu

F32 = jnp.float32
BF16 = jnp.bfloat16

HGRN_HEAD = 128
RWKV_HEAD = 64
LN_EPS = 1e-5
GN_EPS = 64e-5
RMS_EPS = 1e-6
KK_EPS = 1e-12

LANES = 128
SUBLANES = 8
VMEM_LIMIT_BYTES = 56 * 1024 * 1024

ROW_TILE = 512
FF_CHUNK = 256
HGRN_ROWS = 256
HGRN_GROUP = 4
RWKV_CHUNK = 64
RWKV_GROUP = 4
RWKV_STREAMS = 2
RWKV_ROWS = 512
RWKV_LOOKAHEAD = 8
CONV_HALO = 32


def _cparams(*sem):
    return pltpu.CompilerParams(dimension_semantics=sem, vmem_limit_bytes=VMEM_LIMIT_BYTES)


def _dot(a, b):
    return jnp.dot(a, b, preferred_element_type=F32)


def _dot_nt(a, b):
    return lax.dot_general(a, b, (((1,), (1,)), ((), ())), preferred_element_type=F32)


def _dot_tn(a, b):
    return lax.dot_general(a, b, (((0,), (0,)), ((), ())), preferred_element_type=F32)


def _sigmoid(x):
    return 1.0 / (1.0 + jnp.exp(-x))


def _layer_norm(y, g, b):
    mu = jnp.mean(y, axis=-1, keepdims=True)
    yc = y - mu
    var = jnp.mean(yc * yc, axis=-1, keepdims=True)
    return yc * lax.rsqrt(var + LN_EPS) * g + b


def _cumsum_rows(x):
    n = x.shape[0]
    row = lax.broadcasted_iota(jnp.int32, (n, 1), 0)
    s = 1
    while s < n:
        x = x + jnp.where(row >= s, pltpu.roll(x, s, 0), 0.0)
        s *= 2
    return x


def _shift_rows(u, prev, k):
    rows = u.shape[0]
    row = lax.broadcasted_iota(jnp.int32, (rows, 1), 0)
    out = pltpu.roll(u, k, 0)
    for j in range(k):
        out = jnp.where(row == j, prev[SUBLANES - k + j:SUBLANES - k + j + 1], out)
    return out


def _const_spec(shape):
    return pl.BlockSpec(shape, lambda *_: (0,) * len(shape))


class _Layer:
    def __init__(self, stack, index):
        self.stack, self.index, self.shape = stack, index, stack.shape[1:]

    def spec(self):
        index = self.index
        return pl.BlockSpec((None,) + self.shape, lambda *_: (index, 0, 0))


def _halo_index(i, tm, halo):
    return jnp.maximum(i * (tm // halo) - 1, 0)


def _ffn_kernel(x_ref, xh_ref, wup_ref, wgate_ref, cw_ref, cb_ref, wdown_ref, g_ref, b_ref, o_ref, h_ref,
                *, tm, seq, alpha):
    i = pl.program_id(0)
    x = x_ref[...]
    xb = x.astype(BF16)
    xhb = xh_ref[...].astype(BF16)
    at_seq_start = (i * tm) % seq == 0
    dff = wup_ref.shape[1]
    for c in range(dff // FF_CHUNK):
        sl = slice(c * FF_CHUNK, (c + 1) * FF_CHUNK)
        wu = wup_ref[:, sl]
        u = _dot(xb, wu)
        uh = jnp.where(at_seq_start, 0.0, _dot(xhb, wu))
        cw = cw_ref[:, sl]
        conv = cw[2:3] * u + cw[1:2] * _shift_rows(u, uh, 1) + cw[0:1] * _shift_rows(u, uh, 2) + cb_ref[:, sl]
        gelu = 0.5 * conv * (1.0 + lax.erf(conv * (0.5 ** 0.5)))
        h_ref[:, sl] = (gelu * _dot(xb, wgate_ref[:, sl])).astype(BF16)
    f = _dot(h_ref[...], wdown_ref[...])
    o_ref[...] = _layer_norm(alpha * x + f, g_ref[...], b_ref[...])


def _ffn(x, w_up, w_gate, conv_w, conv_b, w_down, ln_g, ln_b, *, seq, alpha):
    n, d = x.shape
    dff = w_up.shape[1]
    tm = min(ROW_TILE, seq)
    return pl.pallas_call(
        functools.partial(_ffn_kernel, tm=tm, seq=seq, alpha=alpha),
        grid=(n // tm,),
        in_specs=[
            pl.BlockSpec((tm, d), lambda i: (i, 0)),
            pl.BlockSpec((SUBLANES, d), lambda i: (_halo_index(i, tm, SUBLANES), 0)),
            w_up.spec(), w_gate.spec(), _const_spec(conv_w.shape), _const_spec((1, dff)),
            w_down.spec(), _const_spec((1, d)), _const_spec((1, d)),
        ],
        out_specs=pl.BlockSpec((tm, d), lambda i: (i, 0)),
        out_shape=jax.ShapeDtypeStruct((n, d), F32),
        scratch_shapes=[pltpu.VMEM((tm, dff), BF16)],
        compiler_params=_cparams("parallel"),
        name="conv_ffn",
    )(x, x, w_up.stack, w_gate.stack, conv_w, conv_b, w_down.stack, ln_g, ln_b)


def _ev_in_kernel(x_ref, w_ref, b_ref, lbl_ref, glu_ref, q_ref, lf_ref, k_ref, v_ref, sg_ref, *, layer_j):
    xb = x_ref[...].astype(BF16)
    width = glu_ref.shape[1]

    def proj(c):
        sl = slice(c * width, (c + 1) * width)
        return _dot(xb, w_ref[:, sl]) + b_ref[:, sl]

    glu_ref[...] = proj(0) * _sigmoid(proj(1))
    q = proj(2)
    q_ref[...] = q * _sigmoid(q)

    logits = lbl_ref[...]
    e = jnp.exp(logits - jnp.max(logits, axis=0, keepdims=True))
    p = e / jnp.sum(e, axis=0, keepdims=True)
    lb = jnp.zeros_like(p[0:1])
    for r in range(1, layer_j + 1):
        lb = lb + p[r:r + 1]

    fz = proj(3)
    en = jnp.exp(-jnp.abs(fz))
    inv = 1.0 / (1.0 + en)
    log_sig = jnp.minimum(fz, 0.0) - jnp.log1p(en)
    sig_neg = jnp.where(fz >= 0.0, en * inv, inv)
    a = jnp.log(lb)
    bterm = jnp.log1p(-lb) + log_sig
    lf_ref[...] = jnp.maximum(a, bterm) + jnp.log1p(jnp.exp(-jnp.abs(a - bterm)))
    k_ref[...] = (1.0 - lb) * sig_neg
    v_ref[...] = proj(4)
    gate = proj(5)
    sg_ref[...] = gate * _sigmoid(gate)


def _ev_in(x, w_in, b_in, lb_logits, *, layer_j, seq):
    n, d = x.shape
    din = w_in.shape[1]
    width = din // 6
    tm = min(ROW_TILE, seq)
    out = jax.ShapeDtypeStruct((n, width), F32)
    return pl.pallas_call(
        functools.partial(_ev_in_kernel, layer_j=layer_j),
        grid=(n // tm,),
        in_specs=[pl.BlockSpec((tm, d), lambda i: (i, 0)), w_in.spec(), _const_spec((1, din)),
                  _const_spec(lb_logits.shape)],
        out_specs=[pl.BlockSpec((tm, width), lambda i: (i, 0))] * 6,
        out_shape=[out] * 6,
        compiler_params=_cparams("parallel"),
        name="ev_in_proj",
    )(x, w_in.stack, b_in, lb_logits)


def _hgrn_kernel(q_ref, lf_ref, k_ref, v_ref, sg_ref, g_ref, o_ref, st_ref, *, rows):
    @pl.when(pl.program_id(2) == 0)
    def _():
        st_ref[...] = jnp.zeros_like(st_ref)

    lanes = HGRN_HEAD
    row = lax.broadcasted_iota(jnp.int32, (rows, 1), 0)
    t_i = lax.broadcasted_iota(jnp.int32, (rows, rows), 0)
    s_i = lax.broadcasted_iota(jnp.int32, (rows, rows), 1)
    level = jnp.where(s_i > t_i, -2, 31 - lax.clz(t_i ^ s_i))

    for h in range(q_ref.shape[1] // lanes):
        ln = slice(h * lanes, (h + 1) * lanes)
        lf = lf_ref[:, ln]
        q = q_ref[:, ln]
        k = k_ref[:, ln]
        v = v_ref[:, ln]
        cum = _cumsum_rows(lf)

        scores = jnp.where(level == -1, _dot_nt(q.astype(BF16), k.astype(BF16)), 0.0)
        b = rows // 2
        while b >= 1:
            if 2 * b >= SUBLANES:
                ends = [jnp.broadcast_to(cum[(2 * j + 1) * b - 1:(2 * j + 1) * b], (2 * b, lanes))
                        for j in range(rows // (2 * b))]
                expo = -jnp.abs(cum - (jnp.concatenate(ends, axis=0) if len(ends) > 1 else ends[0]))
            elif b == 2:
                m = row % 4
                expo = jnp.where(m == 0, pltpu.roll(lf, rows - 1, 0),
                                 jnp.where(m == 1, 0.0, jnp.where(m == 2, lf, lf + pltpu.roll(lf, 1, 0))))
            else:
                expo = jnp.where(row % 2 == 1, lf, 0.0)
            e = jnp.exp(expo)
            s_b = _dot_nt((q * e).astype(BF16), (k * e).astype(BF16))
            scores = jnp.where(level == b.bit_length() - 1, s_b, scores)
            b //= 2

        st = st_ref[h]
        tot = cum[rows - 1:rows]
        o = (_dot(scores.astype(BF16), v.astype(BF16))
             + _dot_nt((q * jnp.exp(cum)).astype(BF16), st.astype(BF16)))
        st_ref[h] = st * jnp.exp(tot) + _dot_tn(v.astype(BF16), (k * jnp.exp(tot - cum)).astype(BF16))

        o = o * lax.rsqrt(jnp.mean(o * o, axis=-1, keepdims=True) + RMS_EPS)
        o_ref[:, ln] = (o * g_ref[:, ln] * sg_ref[:, ln]).astype(o_ref.dtype)


def _hgrn(q, lf, k, v, sg, onorm_g, *, batch, seq):
    n, width = q.shape
    lanes = HGRN_GROUP * HGRN_HEAD
    rows = min(HGRN_ROWS, seq)
    steps = seq // rows
    blk = pl.BlockSpec((rows, lanes), lambda b, h, t: (b * steps + t, h))
    return pl.pallas_call(
        functools.partial(_hgrn_kernel, rows=rows),
        grid=(batch, width // lanes, steps),
        in_specs=[blk] * 5 + [pl.BlockSpec((1, lanes), lambda b, h, t: (0, h))],
        out_specs=blk,
        out_shape=jax.ShapeDtypeStruct((n, width), BF16),
        scratch_shapes=[pltpu.VMEM((HGRN_GROUP, HGRN_HEAD, HGRN_HEAD), F32)],
        compiler_params=_cparams("parallel", "parallel", "arbitrary"),
        name="hgrn2",
    )(q, lf, k, v, sg, onorm_g)


def _ev_out_kernel(x_ref, glu_ref, gh_ref, ob_ref, cw_ref, cb_ref, cg_ref, cbeta_ref, wout_ref, g_ref, b_ref,
                   o_ref, *, tm, seq, alpha):
    i = pl.program_id(0)
    at_seq_start = (i * tm) % seq == 0
    taps, width = cw_ref.shape
    g = jnp.concatenate([jnp.where(at_seq_start, 0.0, gh_ref[...]), glu_ref[...]], axis=0)
    span = CONV_HALO + tm
    first = CONV_HALO - (taps - 1)
    acc = jnp.zeros((tm, width), F32) + cb_ref[...]
    for r in range(SUBLANES):
        gr = g if r == 0 else pltpu.roll(g, span - r, 0)
        for base in range(0, CONV_HALO + 1, SUBLANES):
            j = base + r - first
            if 0 <= j < taps:
                acc = acc + cw_ref[j:j + 1, :] * gr[base:base + tm]
    ua = _layer_norm(acc, cg_ref[...], cbeta_ref[...])
    ua = ua * _sigmoid(ua)
    y = _dot(ua.astype(BF16), wout_ref[0:width, :]) + _dot(ob_ref[...].astype(BF16), wout_ref[width:, :])
    o_ref[...] = _layer_norm(alpha * x_ref[...] + y, g_ref[...], b_ref[...])


def _ev_out(x, glu, ob, conv_w, conv_b, cln_g, cln_b, w_out, ln_g, ln_b, *, seq, alpha):
    n, d = x.shape
    width = glu.shape[1]
    tm = min(ROW_TILE, seq)
    return pl.pallas_call(
        functools.partial(_ev_out_kernel, tm=tm, seq=seq, alpha=alpha),
        grid=(n // tm,),
        in_specs=[
            pl.BlockSpec((tm, d), lambda i: (i, 0)),
            pl.BlockSpec((tm, width), lambda i: (i, 0)),
            pl.BlockSpec((CONV_HALO, width), lambda i: (_halo_index(i, tm, CONV_HALO), 0)),
            pl.BlockSpec((tm, width), lambda i: (i, 0)),
            _const_spec(conv_w.shape), _const_spec((1, width)), _const_spec((1, width)), _const_spec((1, width)),
            w_out.spec(), _const_spec((1, d)), _const_spec((1, d)),
        ],
        out_specs=pl.BlockSpec((tm, d), lambda i: (i, 0)),
        out_shape=jax.ShapeDtypeStruct((n, d), F32),
        compiler_params=_cparams("parallel"),
        name="ev_out_proj",
    )(x, glu, glu, ob, conv_w, conv_b, cln_g, cln_b, w_out.stack, ln_g, ln_b)


def _rw_in_kernel(x_ref, xh_ref, mu_ref, wr_ref, wk_ref, wv_ref, w0_ref, w1_ref, w2_ref, a0_ref, a1_ref, a2_ref,
                  g1_ref, g2_ref, r_ref, k_ref, v_ref, lw_ref, a_ref, g_ref, *, tm, seq):
    i = pl.program_id(0)
    x = x_ref[...]
    prev = jnp.where((i * tm) % seq == 0, 0.0, xh_ref[...])
    xx = _shift_rows(x, prev, 1) - x

    def mix(j):
        return (x + xx * mu_ref[j:j + 1, :]).astype(BF16)

    r_ref[...] = _dot(mix(0), wr_ref[...])
    z = w0_ref[...] + _dot(jnp.tanh(_dot(mix(1), w1_ref[...])).astype(BF16), w2_ref[...])
    lw_ref[...] = -(_sigmoid(z) * (jnp.exp(-0.5)))
    k_ref[...] = _dot(mix(2), wk_ref[...])
    v_ref[...] = _dot(mix(3), wv_ref[...])
    a_ref[...] = _sigmoid(a0_ref[...] + _dot(_dot(mix(4), a1_ref[...]).astype(BF16), a2_ref[...]))
    g_ref[...] = _dot(_sigmoid(_dot(mix(5), g1_ref[...])).astype(BF16), g2_ref[...])


def _rw_in(x, mu, w_r, w_k, w_v, w0, w1, w2, a0, a1, a2, g1, g2, *, seq):
    n, d = x.shape
    tm = min(ROW_TILE, seq)
    out = jax.ShapeDtypeStruct((n, d), F32)
    consts = (mu, w_r, w_k, w_v, w0, w1, w2, a0, a1, a2, g1, g2)
    return pl.pallas_call(
        functools.partial(_rw_in_kernel, tm=tm, seq=seq),
        grid=(n // tm,),
        in_specs=[pl.BlockSpec((tm, d), lambda i: (i, 0)),
                  pl.BlockSpec((SUBLANES, d), lambda i: (_halo_index(i, tm, SUBLANES), 0))]
                 + [c.spec() if isinstance(c, _Layer) else _const_spec(c.shape) for c in consts],
        out_specs=[pl.BlockSpec((tm, d), lambda i: (i, 0))] * 6,
        out_shape=[out] * 6,
        compiler_params=_cparams("parallel"),
        name="rwkv_in_proj",
    )(x, x, *[c.stack if isinstance(c, _Layer) else c for c in consts])


def _group_sums(x, ones_bd):
    return _dot(x.astype(BF16), ones_bd)


def _rwkv_kernel(r_ref, k_ref, v_ref, lw_ref, a_ref, g_ref, kk_ref, ka_ref, rk_ref, gng_ref, gnb_ref, o_ref,
                 st_ref, *, rows):
    @pl.when(pl.program_id(2) == 0)
    def _():
        st_ref[...] = jnp.zeros_like(st_ref)

    ch = RWKV_CHUNK
    hd = RWKV_HEAD
    assert ch == hd, "folded matrices and head operands share one block-diagonal mask"
    heads = RWKV_GROUP
    width = heads * hd
    streams = r_ref.shape[1] // width
    r_i = lax.broadcasted_iota(jnp.int32, (width, width), 0)
    c_i = lax.broadcasted_iota(jnp.int32, (width, width), 1)
    same_head = (r_i // hd) == (c_i // hd)
    ones_bd = jnp.where(same_head, 1.0, 0.0).astype(BF16)
    t_i = lax.broadcasted_iota(jnp.int32, (ch, width), 0)
    s_i = lax.broadcasted_iota(jnp.int32, (ch, width), 1) % ch
    strict = t_i > s_i
    incl = t_i >= s_i
    eye = jnp.where(t_i == s_i, 1.0, 0.0)
    zero_b = jnp.zeros((width, width), BF16)

    def stack(x):
        xb = x.astype(BF16)
        return jnp.where(same_head, jnp.concatenate([xb] * heads, axis=0), zero_b)

    def stack_t(x):
        xt = x.T.astype(BF16)
        return jnp.where(same_head, jnp.concatenate([xt] * heads, axis=1), zero_b)

    n_chunks = rows // ch
    sl = [slice(c * ch, (c + 1) * ch) for c in range(n_chunks)]

    def stream(si):
        ln = slice(si * width, (si + 1) * width)
        r = r_ref[:, ln]
        k = k_ref[:, ln]
        v = v_ref[:, ln]
        lw = lw_ref[:, ln]
        a = a_ref[:, ln]

        kkr = k * kk_ref[:, ln]
        kk = kkr / jnp.maximum(jnp.sqrt(_group_sums(kkr * kkr, ones_bd)), KK_EPS)
        kp = k * (1.0 + (a - 1.0) * ka_ref[:, ln])
        beta = kk * a

        r_t, a_t, kb, dec, a_rb, arkv, ta, vt = ({} for _ in range(8))
        g_in, vu_t, ys = {}, {}, {}

        def prepare(cs):
            p_k, p_b, l_ab, tinv, lp, lv = ({} for _ in range(6))
            for c in cs:
                lwc = lw[sl[c]]
                cum = _cumsum_rows(lwc)
                tot = cum[ch - 1:ch]
                e_neg = jnp.exp(-cum)
                e_out = jnp.exp(tot - cum)
                r_t[c] = r[sl[c]] * jnp.exp(cum)
                a_t[c] = -kk[sl[c]] * jnp.exp(cum - lwc)
                kb[c] = jnp.concatenate([kp[sl[c]] * e_out, beta[sl[c]] * e_out], axis=0).astype(BF16)
                dec[c] = jnp.exp(tot)
                ar = jnp.concatenate([a_t[c], r_t[c]], axis=0).astype(BF16)
                p_k[c] = _dot(ar, stack_t(kp[sl[c]] * e_neg))
                p_b[c] = _dot(ar, stack_t(beta[sl[c]] * e_neg))
            yield
            for c in cs:
                l_ab[c] = jnp.where(strict, p_b[c][0:ch], 0.0)
                a_rb[c] = jnp.where(incl, p_b[c][ch:], 0.0).astype(BF16)
                tinv[c] = eye + l_ab[c]
                lp[c] = _dot(l_ab[c].astype(BF16), stack(l_ab[c]))
            yield
            for c in cs:
                lk = jnp.concatenate([jnp.where(strict, p_k[c][0:ch], 0.0), jnp.where(incl, p_k[c][ch:], 0.0)],
                                     axis=0)
                z = _dot(lk.astype(BF16), stack(v[sl[c]]))
                lv[c] = z[0:ch]
                arkv[c] = z[ch:]
            yield
            n_sq = ch.bit_length() - 2
            for s in range(1, n_sq + 1):
                for c in cs:
                    lp_bd = stack(lp[c])
                    if s < n_sq:
                        z = _dot(jnp.concatenate([tinv[c], lp[c]], axis=0).astype(BF16), lp_bd)
                        tinv[c] = tinv[c] + z[0:ch]
                        lp[c] = z[ch:]
                    else:
                        tinv[c] = tinv[c] + _dot(tinv[c].astype(BF16), lp_bd)
                yield
            for c in cs:
                tb = tinv[c].astype(BF16)
                t_a = _dot(tb, stack(a_t[c]))
                tlv = _dot(tb, stack(lv[c]))
                ta[c] = jnp.concatenate([jnp.zeros_like(t_a), t_a], axis=0).astype(BF16)
                vt[c] = jnp.concatenate([v[sl[c]], tlv], axis=0).T
            yield

        state = [st_ref[si]]

        def advance(cs):
            for c in cs:
                g_in[c] = state[0]
                vu_t[c] = _dot_nt(g_in[c].astype(BF16), ta[c]) + vt[c]
                yield
                state[0] = state[0] * dec[c] + jnp.where(same_head, _dot(vu_t[c].astype(BF16), kb[c]), 0.0)
                yield

        def finish(cs):
            for c in cs:
                y_g = _dot(r_t[c].astype(BF16), g_in[c].T.astype(BF16))
                yield
                ys[c] = y_g + arkv[c] + _dot(a_rb[c], stack(vu_t[c][:, ch:].T))
                yield

        def finalize():
            st_ref[si] = state[0]
            y = jnp.concatenate([ys[c] for c in range(n_chunks)], axis=0)
            inv_hd = 1.0 / hd
            yc = y - _group_sums(y, ones_bd) * inv_hd
            var = _group_sums(yc * yc, ones_bd) * inv_hd
            yn = yc * lax.rsqrt(var + GN_EPS) * gng_ref[:, ln] + gnb_ref[:, ln]
            bonus = _group_sums(r * kp * rk_ref[:, ln], ones_bd) * v
            o_ref[:, ln] = ((yn + bonus) * g_ref[:, ln]).astype(o_ref.dtype)

        return prepare, advance, finish, finalize

    parts = [stream(si) for si in range(streams)]
    groups = [list(range(i, min(i + RWKV_LOOKAHEAD, n_chunks))) for i in range(0, n_chunks, RWKV_LOOKAHEAD)]

    def lockstep(gens):
        alive = False
        for g in gens:
            try:
                next(g)
                alive = True
            except StopIteration:
                pass
        return alive

    def drain(gens):
        while lockstep(gens):
            pass

    drain([p[0](groups[0]) for p in parts])
    for gi, grp in enumerate(groups):
        nxt = [p[0](groups[gi + 1]) for p in parts] if gi + 1 < len(groups) else []
        prv = [p[2](groups[gi - 1]) for p in parts] if gi > 0 else []
        adv = [p[1](grp) for p in parts]
        while lockstep(adv):
            lockstep(nxt)
            lockstep(prv)
        drain(nxt)
        drain(prv)
    drain([p[2](groups[-1]) for p in parts])
    for p in parts:
        p[3]()


def _rwkv(r, k, v, lw, a, g, k_k, k_a, r_k, gn_g, gn_b, *, batch, seq):
    n, d = r.shape
    group = RWKV_GROUP * RWKV_HEAD
    width = RWKV_STREAMS * group
    rows = min(RWKV_ROWS, seq)
    steps = seq // rows
    blk = pl.BlockSpec((rows, width), lambda b, h, t: (b * steps + t, h))
    par = pl.BlockSpec((1, width), lambda b, h, t: (0, h))
    return pl.pallas_call(
        functools.partial(_rwkv_kernel, rows=rows),
        grid=(batch, d // width, steps),
        in_specs=[blk] * 6 + [par] * 5,
        out_specs=blk,
        out_shape=jax.ShapeDtypeStruct((n, d), BF16),
        scratch_shapes=[pltpu.VMEM((RWKV_STREAMS, group, group), F32)],
        compiler_params=_cparams("parallel", "parallel", "arbitrary"),
        name="rwkv7",
    )(r, k, v, lw, a, g, k_k, k_a, r_k, gn_g, gn_b)


def _proj_norm_kernel(x_ref, z_ref, w_ref, g_ref, b_ref, o_ref, *, alpha):
    y = _dot(z_ref[...].astype(BF16), w_ref[...])
    o_ref[...] = _layer_norm(alpha * x_ref[...] + y, g_ref[...], b_ref[...])


def _proj_norm(x, z, w, ln_g, ln_b, *, seq, alpha):
    n, d = x.shape
    tm = min(ROW_TILE, seq)
    return pl.pallas_call(
        functools.partial(_proj_norm_kernel, alpha=alpha),
        grid=(n // tm,),
        in_specs=[pl.BlockSpec((tm, d), lambda i: (i, 0)), pl.BlockSpec((tm, d), lambda i: (i, 0)),
                  w.spec(), _const_spec((1, d)), _const_spec((1, d))],
        out_specs=pl.BlockSpec((tm, d), lambda i: (i, 0)),
        out_shape=jax.ShapeDtypeStruct((n, d), F32),
        compiler_params=_cparams("parallel"),
        name="rw_out_proj",
    )(x, z, w.stack, ln_g, ln_b)


def _pad_cols(w, mult):
    pad = (-w.shape[-1]) % mult
    return jnp.pad(w, ((0, 0),) * (w.ndim - 1) + ((0, pad),))


def _pad_rows(w, mult):
    pad = (-w.shape[-2]) % mult
    return jnp.pad(w, ((0, 0),) * (w.ndim - 2) + ((0, pad), (0, 0)))


def kernel(x, ln_mix_g, ln_mix_b, ln_ffn_g, ln_ffn_b, ev_w_in, ev_b_in, ev_conv_w, ev_conv_b, ev_cln_g, ev_cln_b, ev_lb_logits, ev_onorm_g, ev_w_out, rw_mu, rw_w_r, rw_w_k, rw_w_v, rw_w_o, rw_w0, rw_w1, rw_w2, rw_a0, rw_a1, rw_a2, rw_g1, rw_g2, rw_k_k, rw_k_a, rw_r_k, rw_gn_g, rw_gn_b, ff_w_up, ff_w_gate, ff_conv_w, ff_conv_b, ff_w_down):
    batch, seq, d = x.shape
    depth = ln_mix_g.shape[0]
    alpha = (2 * depth) ** 0.25
    row = lambda p: p.reshape(1, -1)
    bf = lambda w: w.astype(BF16)
    ev_w_in, ev_w_out = bf(ev_w_in), bf(ev_w_out)
    rw_w_r, rw_w_k, rw_w_v, rw_w_o = bf(rw_w_r), bf(rw_w_k), bf(rw_w_v), bf(rw_w_o)
    rw_w1, rw_a1, rw_g1 = (bf(_pad_cols(w, LANES)) for w in (rw_w1, rw_a1, rw_g1))
    rw_w2, rw_a2, rw_g2 = (bf(_pad_rows(w, LANES)) for w in (rw_w2, rw_a2, rw_g2))
    ff_w_up, ff_w_gate, ff_w_down = bf(ff_w_up), bf(ff_w_gate), bf(ff_w_down)
    h = x.reshape(batch * seq, d)
    for layer in range(depth):
        j = layer // 2
        if layer % 2 == 0:
            glu, q, lf, k, v, sg = _ev_in(h, _Layer(ev_w_in, j), row(ev_b_in[j]), ev_lb_logits, layer_j=j, seq=seq)
            ob = _hgrn(q, lf, k, v, sg, row(ev_onorm_g[j]), batch=batch, seq=seq)
            h = _ev_out(h, glu, ob, ev_conv_w[j], row(ev_conv_b[j]), row(ev_cln_g[j]), row(ev_cln_b[j]),
                        _Layer(ev_w_out, j), row(ln_mix_g[layer]), row(ln_mix_b[layer]), seq=seq, alpha=alpha)
        else:
            r, k, v, lw, a, g = _rw_in(
                h, rw_mu[j], _Layer(rw_w_r, j), _Layer(rw_w_k, j), _Layer(rw_w_v, j), row(rw_w0[j]),
                _Layer(rw_w1, j), _Layer(rw_w2, j), row(rw_a0[j]), _Layer(rw_a1, j), _Layer(rw_a2, j),
                _Layer(rw_g1, j), _Layer(rw_g2, j), seq=seq)
            z = _rwkv(r, k, v, lw, a, g, row(rw_k_k[j]), row(rw_k_a[j]), row(rw_r_k[j]), row(rw_gn_g[j]),
                      row(rw_gn_b[j]), batch=batch, seq=seq)
            h = _proj_norm(h, z, _Layer(rw_w_o, j), row(ln_mix_g[layer]), row(ln_mix_b[layer]), seq=seq,
                           alpha=alpha)
        h = _ffn(h, _Layer(ff_w_up, layer), _Layer(ff_w_gate, layer), ff_conv_w[layer], row(ff_conv_b[layer]),
                 _Layer(ff_w_down, layer), row(ln_ffn_g[layer]), row(ln_ffn_b[layer]), seq=seq, alpha=alpha)
    return h.reshape(batch, seq, d)
```

```python
import functools

import jax
import jax.numpy as jnp
from jax import lax
from jax.experimental import pallas as pl
from jax.experimental.pallas import tpu as pltpu

F32 = jnp.float32
BF16 = jnp.bfloat16

HGRN_HEAD = 128
RWKV_HEAD = 64
LN_EPS = 1e-5
GN_EPS = 64e-5
RMS_EPS = 1e-6
KK_EPS = 1e-12

LANES = 128
SUBLANES = 8
VMEM_LIMIT_BYTES = 56 * 1024 * 1024

ROW_TILE = 512
FF_CHUNK = 256
HGRN_ROWS = 256
HGRN_GROUP = 4
RWKV_CHUNK = 64
RWKV_GROUP = 4
RWKV_STREAMS = 2
RWKV_ROWS = 1024
RWKV_LOOKAHEAD = 8
CONV_HALO = 32


def _cparams(*sem):
    return pltpu.CompilerParams(dimension_semantics=sem, vmem_limit_bytes=VMEM_LIMIT_BYTES)


def _dot(a, b):
    return jnp.dot(a, b, preferred_element_type=F32)


def _dot_nt(a, b):
    return lax.dot_general(a, b, (((1,), (1,)), ((), ())), preferred_element_type=F32)


def _dot_tn(a, b):
    return lax.dot_general(a, b, (((0,), (0,)), ((), ())), preferred_element_type=F32)


def _sigmoid(x):
    return 1.0 / (1.0 + jnp.exp(-x))


def _layer_norm(y, g, b):
    mu = jnp.mean(y, axis=-1, keepdims=True)
    yc = y - mu
    var = jnp.mean(yc * yc, axis=-1, keepdims=True)
    return yc * lax.rsqrt(var + LN_EPS) * g + b


def _cumsum_rows(x):
    n = x.shape[0]
    row = lax.broadcasted_iota(jnp.int32, (n, 1), 0)
    s = 1
    while s < n:
        x = x + jnp.where(row >= s, pltpu.roll(x, s, 0), 0.0)
        s *= 2
    return x


def _shift_rows(u, prev, k):
    rows = u.shape[0]
    row = lax.broadcasted_iota(jnp.int32, (rows, 1), 0)
    out = pltpu.roll(u, k, 0)
    for j in range(k):
        out = jnp.where(row == j, prev[SUBLANES - k + j:SUBLANES - k + j + 1], out)
    return out


def _const_spec(shape):
    return pl.BlockSpec(shape, lambda *_: (0,) * len(shape))


class _Layer:
    def __init__(self, stack, index):
        self.stack, self.index, self.shape = stack, index, stack.shape[1:]

    def spec(self):
        index = self.index
        return pl.BlockSpec((None,) + self.shape, lambda *_: (index, 0, 0))


def _halo_index(i, tm, halo):
    return jnp.maximum(i * (tm // halo) - 1, 0)


def _ffn_kernel(x_ref, xh_ref, wup_ref, wgate_ref, cw_ref, cb_ref, wdown_ref, g_ref, b_ref, o_ref, h_ref,
                *, tm, seq, alpha):
    i = pl.program_id(0)
    x = x_ref[...]
    xb = x.astype(BF16)
    xhb = xh_ref[...].astype(BF16)
    at_seq_start = (i * tm) % seq == 0
    dff = wup_ref.shape[1]
    for c in range(dff // FF_CHUNK):
        sl = slice(c * FF_CHUNK, (c + 1) * FF_CHUNK)
        wu = wup_ref[:, sl]
        u = _dot(xb, wu)
        uh = jnp.where(at_seq_start, 0.0, _dot(xhb, wu))
        cw = cw_ref[:, sl]
        conv = cw[2:3] * u + cw[1:2] * _shift_rows(u, uh, 1) + cw[0:1] * _shift_rows(u, uh, 2) + cb_ref[:, sl]
        gelu = 0.5 * conv * (1.0 + lax.erf(conv * (0.5 ** 0.5)))
        h_ref[:, sl] = (gelu * _dot(xb, wgate_ref[:, sl])).astype(BF16)
    f = _dot(h_ref[...], wdown_ref[...])
    o_ref[...] = _layer_norm(alpha * x + f, g_ref[...], b_ref[...])


def _ffn(x, w_up, w_gate, conv_w, conv_b, w_down, ln_g, ln_b, *, seq, alpha):
    n, d = x.shape
    dff = w_up.shape[1]
    tm = min(ROW_TILE, seq)
    return pl.pallas_call(
        functools.partial(_ffn_kernel, tm=tm, seq=seq, alpha=alpha),
        grid=(n // tm,),
        in_specs=[
            pl.BlockSpec((tm, d), lambda i: (i, 0)),
            pl.BlockSpec((SUBLANES, d), lambda i: (_halo_index(i, tm, SUBLANES), 0)),
            w_up.spec(), w_gate.spec(), _const_spec(conv_w.shape), _const_spec((1, dff)),
            w_down.spec(), _const_spec((1, d)), _const_spec((1, d)),
        ],
        out_specs=pl.BlockSpec((tm, d), lambda i: (i, 0)),
        out_shape=jax.ShapeDtypeStruct((n, d), F32),
        scratch_shapes=[pltpu.VMEM((tm, dff), BF16)],
        compiler_params=_cparams("parallel"),
        name="conv_ffn",
    )(x, x, w_up.stack, w_gate.stack, conv_w, conv_b, w_down.stack, ln_g, ln_b)


def _ev_in_kernel(x_ref, w_ref, b_ref, lbl_ref, glu_ref, q_ref, lf_ref, k_ref, v_ref, sg_ref, *, layer_j):
    xb = x_ref[...].astype(BF16)
    width = glu_ref.shape[1]

    def proj(c):
        sl = slice(c * width, (c + 1) * width)
        return _dot(xb, w_ref[:, sl]) + b_ref[:, sl]

    glu_ref[...] = proj(0) * _sigmoid(proj(1))
    q = proj(2)
    q_ref[...] = q * _sigmoid(q)

    logits = lbl_ref[...]
    e = jnp.exp(logits - jnp.max(logits, axis=0, keepdims=True))
    p = e / jnp.sum(e, axis=0, keepdims=True)
    lb = jnp.zeros_like(p[0:1])
    for r in range(1, layer_j + 1):
        lb = lb + p[r:r + 1]

    fz = proj(3)
    en = jnp.exp(-jnp.abs(fz))
    inv = 1.0 / (1.0 + en)
    log_sig = jnp.minimum(fz, 0.0) - jnp.log1p(en)
    sig_neg = jnp.where(fz >= 0.0, en * inv, inv)
    a = jnp.log(lb)
    bterm = jnp.log1p(-lb) + log_sig
    lf_ref[...] = jnp.maximum(a, bterm) + jnp.log1p(jnp.exp(-jnp.abs(a - bterm)))
    k_ref[...] = (1.0 - lb) * sig_neg
    v_ref[...] = proj(4)
    gate = proj(5)
    sg_ref[...] = gate * _sigmoid(gate)


def _ev_in(x, w_in, b_in, lb_logits, *, layer_j, seq):
    n, d = x.shape
    din = w_in.shape[1]
    width = din // 6
    tm = min(ROW_TILE, seq)
    out = jax.ShapeDtypeStruct((n, width), F32)
    return pl.pallas_call(
        functools.partial(_ev_in_kernel, layer_j=layer_j),
        grid=(n // tm,),
        in_specs=[pl.BlockSpec((tm, d), lambda i: (i, 0)), w_in.spec(), _const_spec((1, din)),
                  _const_spec(lb_logits.shape)],
        out_specs=[pl.BlockSpec((tm, width), lambda i: (i, 0))] * 6,
        out_shape=[out] * 6,
        compiler_params=_cparams("parallel"),
        name="ev_in_proj",
    )(x, w_in.stack, b_in, lb_logits)


def _hgrn_kernel(q_ref, lf_ref, k_ref, v_ref, sg_ref, g_ref, o_ref, st_ref, *, rows):
    @pl.when(pl.program_id(2) == 0)
    def _():
        st_ref[...] = jnp.zeros_like(st_ref)

    lanes = HGRN_HEAD
    row = lax.broadcasted_iota(jnp.int32, (rows, 1), 0)
    t_i = lax.broadcasted_iota(jnp.int32, (rows, rows), 0)
    s_i = lax.broadcasted_iota(jnp.int32, (rows, rows), 1)
    level = jnp.where(s_i > t_i, -2, 31 - lax.clz(t_i ^ s_i))

    for h in range(q_ref.shape[1] // lanes):
        ln = slice(h * lanes, (h + 1) * lanes)
        lf = lf_ref[:, ln]
        q = q_ref[:, ln]
        k = k_ref[:, ln]
        v = v_ref[:, ln]
        cum = _cumsum_rows(lf)

        scores = jnp.where(level == -1, _dot_nt(q.astype(BF16), k.astype(BF16)), 0.0)
        b = rows // 2
        while b >= 1:
            if 2 * b >= SUBLANES:
                ends = [jnp.broadcast_to(cum[(2 * j + 1) * b - 1:(2 * j + 1) * b], (2 * b, lanes))
                        for j in range(rows // (2 * b))]
                expo = -jnp.abs(cum - (jnp.concatenate(ends, axis=0) if len(ends) > 1 else ends[0]))
            elif b == 2:
                m = row % 4
                expo = jnp.where(m == 0, pltpu.roll(lf, rows - 1, 0),
                                 jnp.where(m == 1, 0.0, jnp.where(m == 2, lf, lf + pltpu.roll(lf, 1, 0))))
            else:
                expo = jnp.where(row % 2 == 1, lf, 0.0)
            e = jnp.exp(expo)
            s_b = _dot_nt((q * e).astype(BF16), (k * e).astype(BF16))
            scores = jnp.where(level == b.bit_length() - 1, s_b, scores)
            b //= 2

        st = st_ref[h]
        tot = cum[rows - 1:rows]
        o = (_dot(scores.astype(BF16), v.astype(BF16))
             + _dot_nt((q * jnp.exp(cum)).astype(BF16), st.astype(BF16)))
        st_ref[h] = st * jnp.exp(tot) + _dot_tn(v.astype(BF16), (k * jnp.exp(tot - cum)).astype(BF16))

        o = o * lax.rsqrt(jnp.mean(o * o, axis=-1, keepdims=True) + RMS_EPS)
        o_ref[:, ln] = (o * g_ref[:, ln] * sg_ref[:, ln]).astype(o_ref.dtype)


def _hgrn(q, lf, k, v, sg, onorm_g, *, batch, seq):
    n, width = q.shape
    lanes = HGRN_GROUP * HGRN_HEAD
    rows = min(HGRN_ROWS, seq)
    steps = seq // rows
    blk = pl.BlockSpec((rows, lanes), lambda b, h, t: (b * steps + t, h))
    return pl.pallas_call(
        functools.partial(_hgrn_kernel, rows=rows),
        grid=(batch, width // lanes, steps),
        in_specs=[blk] * 5 + [pl.BlockSpec((1, lanes), lambda b, h, t: (0, h))],
        out_specs=blk,
        out_shape=jax.ShapeDtypeStruct((n, width), BF16),
        scratch_shapes=[pltpu.VMEM((HGRN_GROUP, HGRN_HEAD, HGRN_HEAD), F32)],
        compiler_params=_cparams("parallel", "parallel", "arbitrary"),
        name="hgrn2",
    )(q, lf, k, v, sg, onorm_g)


def _ev_out_kernel(x_ref, glu_ref, gh_ref, ob_ref, cw_ref, cb_ref, cg_ref, cbeta_ref, wout_ref, g_ref, b_ref,
                   o_ref, *, tm, seq, alpha):
    i = pl.program_id(0)
    at_seq_start = (i * tm) % seq == 0
    taps, width = cw_ref.shape
    g = jnp.concatenate([jnp.where(at_seq_start, 0.0, gh_ref[...]), glu_ref[...]], axis=0)
    span = CONV_HALO + tm
    first = CONV_HALO - (taps - 1)
    acc = jnp.zeros((tm, width), F32) + cb_ref[...]
    for r in range(SUBLANES):
        gr = g if r == 0 else pltpu.roll(g, span - r, 0)
        for base in range(0, CONV_HALO + 1, SUBLANES):
            j = base + r - first
            if 0 <= j < taps:
                acc = acc + cw_ref[j:j + 1, :] * gr[base:base + tm]
    ua = _layer_norm(acc, cg_ref[...], cbeta_ref[...])
    ua = ua * _sigmoid(ua)
    y = _dot(ua.astype(BF16), wout_ref[0:width, :]) + _dot(ob_ref[...].astype(BF16), wout_ref[width:, :])
    o_ref[...] = _layer_norm(alpha * x_ref[...] + y, g_ref[...], b_ref[...])


def _ev_out(x, glu, ob, conv_w, conv_b, cln_g, cln_b, w_out, ln_g, ln_b, *, seq, alpha):
    n, d = x.shape
    width = glu.shape[1]
    tm = min(ROW_TILE, seq)
    return pl.pallas_call(
        functools.partial(_ev_out_kernel, tm=tm, seq=seq, alpha=alpha),
        grid=(n // tm,),
        in_specs=[
            pl.BlockSpec((tm, d), lambda i: (i, 0)),
            pl.BlockSpec((tm, width), lambda i: (i, 0)),
            pl.BlockSpec((CONV_HALO, width), lambda i: (_halo_index(i, tm, CONV_HALO), 0)),
            pl.BlockSpec((tm, width), lambda i: (i, 0)),
            _const_spec(conv_w.shape), _const_spec((1, width)), _const_spec((1, width)), _const_spec((1, width)),
            w_out.spec(), _const_spec((1, d)), _const_spec((1, d)),
        ],
        out_specs=pl.BlockSpec((tm, d), lambda i: (i, 0)),
        out_shape=jax.ShapeDtypeStruct((n, d), F32),
        compiler_params=_cparams("parallel"),
        name="ev_out_proj",
    )(x, glu, glu, ob, conv_w, conv_b, cln_g, cln_b, w_out.stack, ln_g, ln_b)


def _rw_in_kernel(x_ref, xh_ref, mu_ref, wr_ref, wk_ref, wv_ref, w0_ref, w1_ref, w2_ref, a0_ref, a1_ref, a2_ref,
                  g1_ref, g2_ref, r_ref, k_ref, v_ref, lw_ref, a_ref, g_ref, *, tm, seq):
    i = pl.program_id(0)
    x = x_ref[...]
    prev = jnp.where((i * tm) % seq == 0, 0.0, xh_ref[...])
    xx = _shift_rows(x, prev, 1) - x

    def mix(j):
        return (x + xx * mu_ref[j:j + 1, :]).astype(BF16)

    r_ref[...] = _dot(mix(0), wr_ref[...])
    z = w0_ref[...] + _dot(jnp.tanh(_dot(mix(1), w1_ref[...])).astype(BF16), w2_ref[...])
    lw_ref[...] = -(_sigmoid(z) * (jnp.exp(-0.5)))
    k_ref[...] = _dot(mix(2), wk_ref[...])
    v_ref[...] = _dot(mix(3), wv_ref[...])
    a_ref[...] = _sigmoid(a0_ref[...] + _dot(_dot(mix(4), a1_ref[...]).astype(BF16), a2_ref[...]))
    g_ref[...] = _dot(_sigmoid(_dot(mix(5), g1_ref[...])).astype(BF16), g2_ref[...])


def _rw_in(x, mu, w_r, w_k, w_v, w0, w1, w2, a0, a1, a2, g1, g2, *, seq):
    n, d = x.shape
    tm = min(ROW_TILE, seq)
    out = jax.ShapeDtypeStruct((n, d), F32)
    consts = (mu, w_r, w_k, w_v, w0, w1, w2, a0, a1, a2, g1, g2)
    return pl.pallas_call(
        functools.partial(_rw_in_kernel, tm=tm, seq=seq),
        grid=(n // tm,),
        in_specs=[pl.BlockSpec((tm, d), lambda i: (i, 0)),
                  pl.BlockSpec((SUBLANES, d), lambda i: (_halo_index(i, tm, SUBLANES), 0))]
                 + [c.spec() if isinstance(c, _Layer) else _const_spec(c.shape) for c in consts],
        out_specs=[pl.BlockSpec((tm, d), lambda i: (i, 0))] * 6,
        out_shape=[out] * 6,
        compiler_params=_cparams("parallel"),
        name="rwkv_in_proj",
    )(x, x, *[c.stack if isinstance(c, _Layer) else c for c in consts])


def _group_sums(x, ones_bd):
    return _dot(x.astype(BF16), ones_bd)


def _rwkv_kernel(r_ref, k_ref, v_ref, lw_ref, a_ref, g_ref, kk_ref, ka_ref, rk_ref, gng_ref, gnb_ref, o_ref,
                 st_ref, *, rows):
    @pl.when(pl.program_id(2) == 0)
    def _():
        st_ref[...] = jnp.zeros_like(st_ref)

    ch = RWKV_CHUNK
    hd = RWKV_HEAD
    assert ch == hd, "folded matrices and head operands share one block-diagonal mask"
    heads = RWKV_GROUP
    width = heads * hd
    streams = r_ref.shape[1] // width
    r_i = lax.broadcasted_iota(jnp.int32, (width, width), 0)
    c_i = lax.broadcasted_iota(jnp.int32, (width, width), 1)
    same_head = (r_i // hd) == (c_i // hd)
    ones_bd = jnp.where(same_head, 1.0, 0.0).astype(BF16)
    t_i = lax.broadcasted_iota(jnp.int32, (ch, width), 0)
    s_i = lax.broadcasted_iota(jnp.int32, (ch, width), 1) % ch
    strict = t_i > s_i
    incl = t_i >= s_i
    eye = jnp.where(t_i == s_i, 1.0, 0.0)
    zero_b = jnp.zeros((width, width), BF16)

    def stack(x):
        xb = x.astype(BF16)
        return jnp.where(same_head, jnp.concatenate([xb] * heads, axis=0), zero_b)

    def stack_t(x):
        xt = x.T.astype(BF16)
        return jnp.where(same_head, jnp.concatenate([xt] * heads, axis=1), zero_b)

    n_chunks = rows // ch
    sl = [slice(c * ch, (c + 1) * ch) for c in range(n_chunks)]

    def stream(si):
        ln = slice(si * width, (si + 1) * width)
        r = r_ref[:, ln]
        k = k_ref[:, ln]
        v = v_ref[:, ln]
        lw = lw_ref[:, ln]
        a = a_ref[:, ln]

        kkr = k * kk_ref[:, ln]
        kk = kkr / jnp.maximum(jnp.sqrt(_group_sums(kkr * kkr, ones_bd)), KK_EPS)
        kp = k * (1.0 + (a - 1.0) * ka_ref[:, ln])
        beta = kk * a

        r_t, a_t, kb, dec, a_rb, arkv, ta, vt = ({} for _ in range(8))
        g_in, vu_t, ys = {}, {}, {}

        def prepare(cs):
            p_k, p_b, l_ab, tinv, lp, lv = ({} for _ in range(6))
            for c in cs:
                lwc = lw[sl[c]]
                cum = _cumsum_rows(lwc)
                tot = cum[ch - 1:ch]
                e_neg = jnp.exp(-cum)
                e_out = jnp.exp(tot - cum)
                r_t[c] = r[sl[c]] * jnp.exp(cum)
                a_t[c] = -kk[sl[c]] * jnp.exp(cum - lwc)
                kb[c] = jnp.concatenate([kp[sl[c]] * e_out, beta[sl[c]] * e_out], axis=0).astype(BF16)
                dec[c] = jnp.exp(tot)
                ar = jnp.concatenate([a_t[c], r_t[c]], axis=0).astype(BF16)
                p_k[c] = _dot(ar, stack_t(kp[sl[c]] * e_neg))
                p_b[c] = _dot(ar, stack_t(beta[sl[c]] * e_neg))
            yield
            for c in cs:
                l_ab[c] = jnp.where(strict, p_b[c][0:ch], 0.0)
                a_rb[c] = jnp.where(incl, p_b[c][ch:], 0.0).astype(BF16)
                tinv[c] = eye + l_ab[c]
                lp[c] = _dot(l_ab[c].astype(BF16), stack(l_ab[c]))
            yield
            for c in cs:
                lk = jnp.concatenate([jnp.where(strict, p_k[c][0:ch], 0.0), jnp.where(incl, p_k[c][ch:], 0.0)],
                                     axis=0)
                z = _dot(lk.astype(BF16), stack(v[sl[c]]))
                lv[c] = z[0:ch]
                arkv[c] = z[ch:]
            yield
            n_sq = ch.bit_length() - 2
            for s in range(1, n_sq + 1):
                for c in cs:
                    lp_bd = stack(lp[c])
                    if s < n_sq:
                        z = _dot(jnp.concatenate([tinv[c], lp[c]], axis=0).astype(BF16), lp_bd)
                        tinv[c] = tinv[c] + z[0:ch]
                        lp[c] = z[ch:]
                    else:
                        tinv[c] = tinv[c] + _dot(tinv[c].astype(BF16), lp_bd)
                yield
            for c in cs:
                tb = tinv[c].astype(BF16)
                t_a = _dot(tb, stack(a_t[c]))
                tlv = _dot(tb, stack(lv[c]))
                ta[c] = jnp.concatenate([jnp.zeros_like(t_a), t_a], axis=0).astype(BF16)
                vt[c] = jnp.concatenate([v[sl[c]], tlv], axis=0).T
            yield

        state = [st_ref[si]]

        def advance(cs):
            for c in cs:
                g_in[c] = state[0]
                vu_t[c] = _dot_nt(g_in[c].astype(BF16), ta[c]) + vt[c]
                yield
                state[0] = state[0] * dec[c] + jnp.where(same_head, _dot(vu_t[c].astype(BF16), kb[c]), 0.0)
                yield

        def finish(cs):
            for c in cs:
                y_g = _dot(r_t[c].astype(BF16), g_in[c].T.astype(BF16))
                yield
                ys[c] = y_g + arkv[c] + _dot(a_rb[c], stack(vu_t[c][:, ch:].T))
                yield

        def finalize():
            st_ref[si] = state[0]
            y = jnp.concatenate([ys[c] for c in range(n_chunks)], axis=0)
            inv_hd = 1.0 / hd
            yc = y - _group_sums(y, ones_bd) * inv_hd
            var = _group_sums(yc * yc, ones_bd) * inv_hd
            yn = yc * lax.rsqrt(var + GN_EPS) * gng_ref[:, ln] + gnb_ref[:, ln]
            bonus = _group_sums(r * kp * rk_ref[:, ln], ones_bd) * v
            o_ref[:, ln] = ((yn + bonus) * g_ref[:, ln]).astype(o_ref.dtype)

        return prepare, advance, finish, finalize

    parts = [stream(si) for si in range(streams)]
    groups = [list(range(i, min(i + RWKV_LOOKAHEAD, n_chunks))) for i in range(0, n_chunks, RWKV_LOOKAHEAD)]

    def lockstep(gens):
        alive = False
        for g in gens:
            try:
                next(g)
                alive = True
            except StopIteration:
                pass
        return alive

    def drain(gens):
        while lockstep(gens):
            pass

    drain([p[0](groups[0]) for p in parts])
    for gi, grp in enumerate(groups):
        nxt = [p[0](groups[gi + 1]) for p in parts] if gi + 1 < len(groups) else []
        prv = [p[2](groups[gi - 1]) for p in parts] if gi > 0 else []
        adv = [p[1](grp) for p in parts]
        while lockstep(adv):
            lockstep(nxt)
            lockstep(prv)
        drain(nxt)
        drain(prv)
    drain([p[2](groups[-1]) for p in parts])
    for p in parts:
        p[3]()


def _rwkv(r, k, v, lw, a, g, k_k, k_a, r_k, gn_g, gn_b, *, batch, seq):
    n, d = r.shape
    group = RWKV_GROUP * RWKV_HEAD
    width = RWKV_STREAMS * group
    rows = min(RWKV_ROWS, seq)
    steps = seq // rows
    blk = pl.BlockSpec((rows, width), lambda b, h, t: (b * steps + t, h))
    par = pl.BlockSpec((1, width), lambda b, h, t: (0, h))
    return pl.pallas_call(
        functools.partial(_rwkv_kernel, rows=rows),
        grid=(batch, d // width, steps),
        in_specs=[blk] * 6 + [par] * 5,
        out_specs=blk,
        out_shape=jax.ShapeDtypeStruct((n, d), BF16),
        scratch_shapes=[pltpu.VMEM((RWKV_STREAMS, group, group), F32)],
        compiler_params=_cparams("parallel", "parallel", "arbitrary"),
        name="rwkv7",
    )(r, k, v, lw, a, g, k_k, k_a, r_k, gn_g, gn_b)


def _proj_norm_kernel(x_ref, z_ref, w_ref, g_ref, b_ref, o_ref, *, alpha):
    y = _dot(z_ref[...].astype(BF16), w_ref[...])
    o_ref[...] = _layer_norm(alpha * x_ref[...] + y, g_ref[...], b_ref[...])


def _proj_norm(x, z, w, ln_g, ln_b, *, seq, alpha):
    n, d = x.shape
    tm = min(ROW_TILE, seq)
    return pl.pallas_call(
        functools.partial(_proj_norm_kernel, alpha=alpha),
        grid=(n // tm,),
        in_specs=[pl.BlockSpec((tm, d), lambda i: (i, 0)), pl.BlockSpec((tm, d), lambda i: (i, 0)),
                  w.spec(), _const_spec((1, d)), _const_spec((1, d))],
        out_specs=pl.BlockSpec((tm, d), lambda i: (i, 0)),
        out_shape=jax.ShapeDtypeStruct((n, d), F32),
        compiler_params=_cparams("parallel"),
        name="rw_out_proj",
    )(x, z, w.stack, ln_g, ln_b)


def _pad_cols(w, mult):
    pad = (-w.shape[-1]) % mult
    return jnp.pad(w, ((0, 0),) * (w.ndim - 1) + ((0, pad),))


def _pad_rows(w, mult):
    pad = (-w.shape[-2]) % mult
    return jnp.pad(w, ((0, 0),) * (w.ndim - 2) + ((0, pad), (0, 0)))


def kernel(x, ln_mix_g, ln_mix_b, ln_ffn_g, ln_ffn_b, ev_w_in, ev_b_in, ev_conv_w, ev_conv_b, ev_cln_g, ev_cln_b, ev_lb_logits, ev_onorm_g, ev_w_out, rw_mu, rw_w_r, rw_w_k, rw_w_v, rw_w_o, rw_w0, rw_w1, rw_w2, rw_a0, rw_a1, rw_a2, rw_g1, rw_g2, rw_k_k, rw_k_a, rw_r_k, rw_gn_g, rw_gn_b, ff_w_up, ff_w_gate, ff_conv_w, ff_conv_b, ff_w_down):
    batch, seq, d = x.shape
    depth = ln_mix_g.shape[0]
    alpha = (2 * depth) ** 0.25
    row = lambda p: p.reshape(1, -1)
    bf = lambda w: w.astype(BF16)
    ev_w_in, ev_w_out = bf(ev_w_in), bf(ev_w_out)
    rw_w_r, rw_w_k, rw_w_v, rw_w_o = bf(rw_w_r), bf(rw_w_k), bf(rw_w_v), bf(rw_w_o)
    rw_w1, rw_a1, rw_g1 = (bf(_pad_cols(w, LANES)) for w in (rw_w1, rw_a1, rw_g1))
    rw_w2, rw_a2, rw_g2 = (bf(_pad_rows(w, LANES)) for w in (rw_w2, rw_a2, rw_g2))
    ff_w_up, ff_w_gate, ff_w_down = bf(ff_w_up), bf(ff_w_gate), bf(ff_w_down)
    h = x.reshape(batch * seq, d)
    for layer in range(depth):
        j = layer // 2
        if layer % 2 == 0:
            glu, q, lf, k, v, sg = _ev_in(h, _Layer(ev_w_in, j), row(ev_b_in[j]), ev_lb_logits, layer_j=j, seq=seq)
            ob = _hgrn(q, lf, k, v, sg, row(ev_onorm_g[j]), batch=batch, seq=seq)
            h = _ev_out(h, glu, ob, ev_conv_w[j], row(ev_conv_b[j]), row(ev_cln_g[j]), row(ev_cln_b[j]),
                        _Layer(ev_w_out, j), row(ln_mix_g[layer]), row(ln_mix_b[layer]), seq=seq, alpha=alpha)
        else:
            r, k, v, lw, a, g = _rw_in(
                h, rw_mu[j], _Layer(rw_w_r, j), _Layer(rw_w_k, j), _Layer(rw_w_v, j), row(rw_w0[j]),
                _Layer(rw_w1, j), _Layer(rw_w2, j), row(rw_a0[j]), _Layer(rw_a1, j), _Layer(rw_a2, j),
                _Layer(rw_g1, j), _Layer(rw_g2, j), seq=seq)
            z = _rwkv(r, k, v, lw, a, g, row(rw_k_k[j]), row(rw_k_a[j]), row(rw_r_k[j]), row(rw_gn_g[j]),
                      row(rw_gn_b[j]), batch=batch, seq=seq)
            h = _proj_norm(h, z, _Layer(rw_w_o, j), row(ln_mix_g[layer]), row(ln_mix_b[layer]), seq=seq,
                           alpha=alpha)
        h = _ffn(h, _Layer(ff_w_up, layer), _Layer(ff_w_gate, layer), ff_conv_w[layer], row(ff_conv_b[layer]),
                 _Layer(ff_w_down, layer), row(ln_ffn_g[layer]), row(ln_ffn_b[layer]), seq=seq, alpha=alpha)
    return h.reshape(batch, seq, d)
```

```python
import functools

import jax
import jax.numpy as jnp
from jax import lax
from jax.experimental import pallas as pl
from jax.experimental.pallas import tpu as pltpu

F32 = jnp.float32
BF16 = jnp.bfloat16

HGRN_HEAD = 128
RWKV_HEAD = 64
LN_EPS = 1e-5
GN_EPS = 64e-5
RMS_EPS = 1e-6
KK_EPS = 1e-12

LANES = 128
SUBLANES = 8
VMEM_LIMIT_BYTES = 56 * 1024 * 1024

ROW_TILE = 512
FF_CHUNK = 256
HGRN_ROWS = 256
HGRN_GROUP = 4
RWKV_CHUNK = 64
RWKV_GROUP = 4
RWKV_STREAMS = 2
RWKV_ROWS = 1024
RWKV_LOOKAHEAD = 8
CONV_HALO = 32


def _cparams(*sem):
    return pltpu.CompilerParams(dimension_semantics=sem, vmem_limit_bytes=VMEM_LIMIT_BYTES)


def _dot(a, b):
    return jnp.dot(a, b, preferred_element_type=F32)


def _dot_nt(a, b):
    return lax.dot_general(a, b, (((1,), (1,)), ((), ())), preferred_element_type=F32)


def _dot_tn(a, b):
    return lax.dot_general(a, b, (((0,), (0,)), ((), ())), preferred_element_type=F32)


def _sigmoid(x):
    return 1.0 / (1.0 + jnp.exp(-x))


def _layer_norm(y, g, b):
    mu = jnp.mean(y, axis=-1, keepdims=True)
    yc = y - mu
    var = jnp.mean(yc * yc, axis=-1, keepdims=True)
    return yc * lax.rsqrt(var + LN_EPS) * g + b


def _cumsum_rows(x):
    n = x.shape[0]
    row = lax.broadcasted_iota(jnp.int32, (n, 1), 0)
    s = 1
    while s < n:
        x = x + jnp.where(row >= s, pltpu.roll(x, s, 0), 0.0)
        s *= 2
    return x


def _shift_rows(u, prev, k):
    rows = u.shape[0]
    row = lax.broadcasted_iota(jnp.int32, (rows, 1), 0)
    out = pltpu.roll(u, k, 0)
    for j in range(k):
        out = jnp.where(row == j, prev[SUBLANES - k + j:SUBLANES - k + j + 1], out)
    return out


def _const_spec(shape):
    return pl.BlockSpec(shape, lambda *_: (0,) * len(shape))


class _Layer:
    def __init__(self, stack, index):
        self.stack, self.index, self.shape = stack, index, stack.shape[1:]

    def spec(self):
        index = self.index
        return pl.BlockSpec((None,) + self.shape, lambda *_: (index, 0, 0))


def _halo_index(i, tm, halo):
    return jnp.maximum(i * (tm // halo) - 1, 0)


def _ffn_kernel(x_ref, xh_ref, wup_ref, wgate_ref, cw_ref, cb_ref, wdown_ref, g_ref, b_ref, o_ref, h_ref,
                *, tm, seq, alpha):
    i = pl.program_id(0)
    x = x_ref[...]
    xb = x.astype(BF16)
    xhb = xh_ref[...].astype(BF16)
    at_seq_start = (i * tm) % seq == 0
    dff = wup_ref.shape[1]
    for c in range(dff // FF_CHUNK):
        sl = slice(c * FF_CHUNK, (c + 1) * FF_CHUNK)
        wu = wup_ref[:, sl]
        u = _dot(xb, wu)
        uh = jnp.where(at_seq_start, 0.0, _dot(xhb, wu))
        cw = cw_ref[:, sl]
        conv = cw[2:3] * u + cw[1:2] * _shift_rows(u, uh, 1) + cw[0:1] * _shift_rows(u, uh, 2) + cb_ref[:, sl]
        gelu = 0.5 * conv * (1.0 + lax.erf(conv * (0.5 ** 0.5)))
        h_ref[:, sl] = (gelu * _dot(xb, wgate_ref[:, sl])).astype(BF16)
    f = _dot(h_ref[...], wdown_ref[...])
    o_ref[...] = _layer_norm(alpha * x + f, g_ref[...], b_ref[...])


def _ffn(x, w_up, w_gate, conv_w, conv_b, w_down, ln_g, ln_b, *, seq, alpha):
    n, d = x.shape
    dff = w_up.shape[1]
    tm = min(ROW_TILE, seq)
    return pl.pallas_call(
        functools.partial(_ffn_kernel, tm=tm, seq=seq, alpha=alpha),
        grid=(n // tm,),
        in_specs=[
            pl.BlockSpec((tm, d), lambda i: (i, 0)),
            pl.BlockSpec((SUBLANES, d), lambda i: (_halo_index(i, tm, SUBLANES), 0)),
            w_up.spec(), w_gate.spec(), _const_spec(conv_w.shape), _const_spec((1, dff)),
            w_down.spec(), _const_spec((1, d)), _const_spec((1, d)),
        ],
        out_specs=pl.BlockSpec((tm, d), lambda i: (i, 0)),
        out_shape=jax.ShapeDtypeStruct((n, d), F32),
        scratch_shapes=[pltpu.VMEM((tm, dff), BF16)],
        compiler_params=_cparams("parallel"),
        name="conv_ffn",
    )(x, x, w_up.stack, w_gate.stack, conv_w, conv_b, w_down.stack, ln_g, ln_b)


def _ev_in_kernel(x_ref, w_ref, b_ref, lbl_ref, glu_ref, q_ref, lf_ref, k_ref, v_ref, sg_ref, *, layer_j):
    xb = x_ref[...].astype(BF16)
    width = glu_ref.shape[1]

    def proj(c):
        sl = slice(c * width, (c + 1) * width)
        return _dot(xb, w_ref[:, sl]) + b_ref[:, sl]

    glu_ref[...] = proj(0) * _sigmoid(proj(1))
    q = proj(2)
    q_ref[...] = (q * _sigmoid(q)).astype(q_ref.dtype)

    logits = lbl_ref[...]
    e = jnp.exp(logits - jnp.max(logits, axis=0, keepdims=True))
    p = e / jnp.sum(e, axis=0, keepdims=True)
    lb = jnp.zeros_like(p[0:1])
    for r in range(1, layer_j + 1):
        lb = lb + p[r:r + 1]

    fz = proj(3)
    en = jnp.exp(-jnp.abs(fz))
    inv = 1.0 / (1.0 + en)
    log_sig = jnp.minimum(fz, 0.0) - jnp.log1p(en)
    sig_neg = jnp.where(fz >= 0.0, en * inv, inv)
    a = jnp.log(lb)
    bterm = jnp.log1p(-lb) + log_sig
    lf_ref[...] = jnp.maximum(a, bterm) + jnp.log1p(jnp.exp(-jnp.abs(a - bterm)))
    k_ref[...] = ((1.0 - lb) * sig_neg).astype(k_ref.dtype)
    v_ref[...] = proj(4).astype(v_ref.dtype)
    gate = proj(5)
    sg_ref[...] = (gate * _sigmoid(gate)).astype(sg_ref.dtype)


def _ev_in(x, w_in, b_in, lb_logits, *, layer_j, seq):
    n, d = x.shape
    din = w_in.shape[1]
    width = din // 6
    tm = min(ROW_TILE, seq)
    out = [jax.ShapeDtypeStruct((n, width), dt) for dt in (F32, BF16, F32, BF16, BF16, BF16)]
    return pl.pallas_call(
        functools.partial(_ev_in_kernel, layer_j=layer_j),
        grid=(n // tm,),
        in_specs=[pl.BlockSpec((tm, d), lambda i: (i, 0)), w_in.spec(), _const_spec((1, din)),
                  _const_spec(lb_logits.shape)],
        out_specs=[pl.BlockSpec((tm, width), lambda i: (i, 0))] * 6,
        out_shape=out,
        compiler_params=_cparams("parallel"),
        name="ev_in_proj",
    )(x, w_in.stack, b_in, lb_logits)


def _hgrn_kernel(q_ref, lf_ref, k_ref, v_ref, sg_ref, g_ref, o_ref, st_ref, *, rows):
    @pl.when(pl.program_id(2) == 0)
    def _():
        st_ref[...] = jnp.zeros_like(st_ref)

    lanes = HGRN_HEAD
    row = lax.broadcasted_iota(jnp.int32, (rows, 1), 0)
    t_i = lax.broadcasted_iota(jnp.int32, (rows, rows), 0)
    s_i = lax.broadcasted_iota(jnp.int32, (rows, rows), 1)
    level = jnp.where(s_i > t_i, -2, 31 - lax.clz(t_i ^ s_i))

    for h in range(q_ref.shape[1] // lanes):
        ln = slice(h * lanes, (h + 1) * lanes)
        lf = lf_ref[:, ln]
        q = q_ref[:, ln]
        k = k_ref[:, ln]
        v = v_ref[:, ln]
        cum = _cumsum_rows(lf)

        scores = jnp.where(level == -1, _dot_nt(q.astype(BF16), k.astype(BF16)), 0.0)
        b = rows // 2
        while b >= 1:
            if 2 * b >= SUBLANES:
                ends = [jnp.broadcast_to(cum[(2 * j + 1) * b - 1:(2 * j + 1) * b], (2 * b, lanes))
                        for j in range(rows // (2 * b))]
                expo = -jnp.abs(cum - (jnp.concatenate(ends, axis=0) if len(ends) > 1 else ends[0]))
            elif b == 2:
                m = row % 4
                expo = jnp.where(m == 0, pltpu.roll(lf, rows - 1, 0),
                                 jnp.where(m == 1, 0.0, jnp.where(m == 2, lf, lf + pltpu.roll(lf, 1, 0))))
            else:
                expo = jnp.where(row % 2 == 1, lf, 0.0)
            e = jnp.exp(expo)
            s_b = _dot_nt((q * e).astype(BF16), (k * e).astype(BF16))
            scores = jnp.where(level == b.bit_length() - 1, s_b, scores)
            b //= 2

        st = st_ref[h]
        tot = cum[rows - 1:rows]
        o = (_dot(scores.astype(BF16), v.astype(BF16))
             + _dot_nt((q * jnp.exp(cum)).astype(BF16), st.astype(BF16)))
        st_ref[h] = st * jnp.exp(tot) + _dot_tn(v.astype(BF16), (k * jnp.exp(tot - cum)).astype(BF16))

        o = o * lax.rsqrt(jnp.mean(o * o, axis=-1, keepdims=True) + RMS_EPS)
        o_ref[:, ln] = (o * g_ref[:, ln] * sg_ref[:, ln]).astype(o_ref.dtype)


def _hgrn(q, lf, k, v, sg, onorm_g, *, batch, seq):
    n, width = q.shape
    lanes = HGRN_GROUP * HGRN_HEAD
    rows = min(HGRN_ROWS, seq)
    steps = seq // rows
    blk = pl.BlockSpec((rows, lanes), lambda b, h, t: (b * steps + t, h))
    return pl.pallas_call(
        functools.partial(_hgrn_kernel, rows=rows),
        grid=(batch, width // lanes, steps),
        in_specs=[blk] * 5 + [pl.BlockSpec((1, lanes), lambda b, h, t: (0, h))],
        out_specs=blk,
        out_shape=jax.ShapeDtypeStruct((n, width), BF16),
        scratch_shapes=[pltpu.VMEM((HGRN_GROUP, HGRN_HEAD, HGRN_HEAD), F32)],
        compiler_params=_cparams("parallel", "parallel", "arbitrary"),
        name="hgrn2",
    )(q, lf, k, v, sg, onorm_g)


def _ev_out_kernel(x_ref, glu_ref, gh_ref, ob_ref, cw_ref, cb_ref, cg_ref, cbeta_ref, wout_ref, g_ref, b_ref,
                   o_ref, *, tm, seq, alpha):
    i = pl.program_id(0)
    at_seq_start = (i * tm) % seq == 0
    taps, width = cw_ref.shape
    g = jnp.concatenate([jnp.where(at_seq_start, 0.0, gh_ref[...]), glu_ref[...]], axis=0)
    span = CONV_HALO + tm
    first = CONV_HALO - (taps - 1)
    acc = jnp.zeros((tm, width), F32) + cb_ref[...]
    for r in range(SUBLANES):
        gr = g if r == 0 else pltpu.roll(g, span - r, 0)
        for base in range(0, CONV_HALO + 1, SUBLANES):
            j = base + r - first
            if 0 <= j < taps:
                acc = acc + cw_ref[j:j + 1, :] * gr[base:base + tm]
    ua = _layer_norm(acc, cg_ref[...], cbeta_ref[...])
    ua = ua * _sigmoid(ua)
    y = _dot(ua.astype(BF16), wout_ref[0:width, :]) + _dot(ob_ref[...].astype(BF16), wout_ref[width:, :])
    o_ref[...] = _layer_norm(alpha * x_ref[...] + y, g_ref[...], b_ref[...])


def _ev_out(x, glu, ob, conv_w, conv_b, cln_g, cln_b, w_out, ln_g, ln_b, *, seq, alpha):
    n, d = x.shape
    width = glu.shape[1]
    tm = min(ROW_TILE, seq)
    return pl.pallas_call(
        functools.partial(_ev_out_kernel, tm=tm, seq=seq, alpha=alpha),
        grid=(n // tm,),
        in_specs=[
            pl.BlockSpec((tm, d), lambda i: (i, 0)),
            pl.BlockSpec((tm, width), lambda i: (i, 0)),
            pl.BlockSpec((CONV_HALO, width), lambda i: (_halo_index(i, tm, CONV_HALO), 0)),
            pl.BlockSpec((tm, width), lambda i: (i, 0)),
            _const_spec(conv_w.shape), _const_spec((1, width)), _const_spec((1, width)), _const_spec((1, width)),
            w_out.spec(), _const_spec((1, d)), _const_spec((1, d)),
        ],
        out_specs=pl.BlockSpec((tm, d), lambda i: (i, 0)),
        out_shape=jax.ShapeDtypeStruct((n, d), F32),
        compiler_params=_cparams("parallel"),
        name="ev_out_proj",
    )(x, glu, glu, ob, conv_w, conv_b, cln_g, cln_b, w_out.stack, ln_g, ln_b)


def _rw_in_kernel(x_ref, xh_ref, mu_ref, wr_ref, wk_ref, wv_ref, w0_ref, w1_ref, w2_ref, a0_ref, a1_ref, a2_ref,
                  g1_ref, g2_ref, r_ref, k_ref, v_ref, lw_ref, a_ref, g_ref, *, tm, seq):
    i = pl.program_id(0)
    x = x_ref[...]
    prev = jnp.where((i * tm) % seq == 0, 0.0, xh_ref[...])
    xx = _shift_rows(x, prev, 1) - x

    def mix(j):
        return (x + xx * mu_ref[j:j + 1, :]).astype(BF16)

    r_ref[...] = _dot(mix(0), wr_ref[...]).astype(r_ref.dtype)
    z = w0_ref[...] + _dot(jnp.tanh(_dot(mix(1), w1_ref[...])).astype(BF16), w2_ref[...])
    lw_ref[...] = -(_sigmoid(z) * (jnp.exp(-0.5)))
    k_ref[...] = _dot(mix(2), wk_ref[...]).astype(k_ref.dtype)
    v_ref[...] = _dot(mix(3), wv_ref[...]).astype(v_ref.dtype)
    a_ref[...] = _sigmoid(a0_ref[...] + _dot(_dot(mix(4), a1_ref[...]).astype(BF16), a2_ref[...]))
    g_ref[...] = _dot(_sigmoid(_dot(mix(5), g1_ref[...])).astype(BF16), g2_ref[...]).astype(g_ref.dtype)


def _rw_in(x, mu, w_r, w_k, w_v, w0, w1, w2, a0, a1, a2, g1, g2, *, seq):
    n, d = x.shape
    tm = min(ROW_TILE, seq)
    out = [jax.ShapeDtypeStruct((n, d), dt) for dt in (BF16, BF16, BF16, F32, F32, BF16)]
    consts =(mu, w_r, w_k, w_v, w0, w1, w2, a0, a1, a2, g1, g2)
    return pl.pallas_call(
        functools.partial(_rw_in_kernel, tm=tm, seq=seq),
        grid=(n // tm,),
        in_specs=[pl.BlockSpec((tm, d), lambda i: (i, 0)),
                  pl.BlockSpec((SUBLANES, d), lambda i: (_halo_index(i, tm, SUBLANES), 0))]
                 + [c.spec() if isinstance(c, _Layer) else _const_spec(c.shape) for c in consts],
        out_specs=[pl.BlockSpec((tm, d), lambda i: (i, 0))] * 6,
        out_shape=out,
        compiler_params=_cparams("parallel"),
        name="rwkv_in_proj",
    )(x, x, *[c.stack if isinstance(c, _Layer) else c for c in consts])


def _group_sums(x, ones_bd):
    return _dot(x.astype(BF16), ones_bd)


def _rwkv_kernel(r_ref, k_ref, v_ref, lw_ref, a_ref, g_ref, kk_ref, ka_ref, rk_ref, gng_ref, gnb_ref, o_ref,
                 st_ref, *, rows):
    @pl.when(pl.program_id(2) == 0)
    def _():
        st_ref[...] = jnp.zeros_like(st_ref)

    ch = RWKV_CHUNK
    hd = RWKV_HEAD
    assert ch == hd, "folded matrices and head operands share one block-diagonal mask"
    heads = RWKV_GROUP
    width = heads * hd
    streams = r_ref.shape[1] // width
    r_i = lax.broadcasted_iota(jnp.int32, (width, width), 0)
    c_i = lax.broadcasted_iota(jnp.int32, (width, width), 1)
    same_head = (r_i // hd) == (c_i // hd)
    ones_bd = jnp.where(same_head, 1.0, 0.0).astype(BF16)
    t_i = lax.broadcasted_iota(jnp.int32, (ch, width), 0)
    s_i = lax.broadcasted_iota(jnp.int32, (ch, width), 1) % ch
    strict = t_i > s_i
    incl = t_i >= s_i
    eye = jnp.where(t_i == s_i, 1.0, 0.0)
    zero_b = jnp.zeros((width, width), BF16)

    def stack(x):
        xb = x.astype(BF16)
        return jnp.where(same_head, jnp.concatenate([xb] * heads, axis=0), zero_b)

    def stack_t(x):
        xt = x.T.astype(BF16)
        return jnp.where(same_head, jnp.concatenate([xt] * heads, axis=1), zero_b)

    n_chunks = rows // ch
    sl = [slice(c * ch, (c + 1) * ch) for c in range(n_chunks)]

    def stream(si):
        ln = slice(si * width, (si + 1) * width)
        r = r_ref[:, ln]
        k = k_ref[:, ln]
        v = v_ref[:, ln]
        lw = lw_ref[:, ln]
        a = a_ref[:, ln]

        kkr = k * kk_ref[:, ln]
        kk = kkr / jnp.maximum(jnp.sqrt(_group_sums(kkr * kkr, ones_bd)), KK_EPS)
        kp = k * (1.0 + (a - 1.0) * ka_ref[:, ln])
        beta = kk * a

        r_t, a_t, kb, dec, a_rb, arkv, ta, vt = ({} for _ in range(8))
        g_in, vu_t, ys = {}, {}, {}

        def prepare(cs):
            p_k, p_b, l_ab, tinv, lp, lv = ({} for _ in range(6))
            for c in cs:
                lwc = lw[sl[c]]
                cum = _cumsum_rows(lwc)
                tot = cum[ch - 1:ch]
                e_neg = jnp.exp(-cum)
                e_out = jnp.exp(tot - cum)
                r_t[c] = r[sl[c]] * jnp.exp(cum)
                a_t[c] = -kk[sl[c]] * jnp.exp(cum - lwc)
                kb[c] = jnp.concatenate([kp[sl[c]] * e_out, beta[sl[c]] * e_out], axis=0).astype(BF16)
                dec[c] = jnp.exp(tot)
                ar = jnp.concatenate([a_t[c], r_t[c]], axis=0).astype(BF16)
                p_k[c] = _dot(ar, stack_t(kp[sl[c]] * e_neg))
                p_b[c] = _dot(ar, stack_t(beta[sl[c]] * e_neg))
            yield
            for c in cs:
                l_ab[c] = jnp.where(strict, p_b[c][0:ch], 0.0)
                a_rb[c] = jnp.where(incl, p_b[c][ch:], 0.0).astype(BF16)
                tinv[c] = eye + l_ab[c]
                lp[c] = _dot(l_ab[c].astype(BF16), stack(l_ab[c]))
            yield
            for c in cs:
                lk = jnp.concatenate([jnp.where(strict, p_k[c][0:ch], 0.0), jnp.where(incl, p_k[c][ch:], 0.0)],
                                     axis=0)
                z = _dot(lk.astype(BF16), stack(v[sl[c]]))
                lv[c] = z[0:ch]
                arkv[c] = z[ch:]
            yield
            n_sq = ch.bit_length() - 2
            for s in range(1, n_sq + 1):
                for c in cs:
                    lp_bd = stack(lp[c])
                    if s < n_sq:
                        z = _dot(jnp.concatenate([tinv[c], lp[c]], axis=0).astype(BF16), lp_bd)
                        tinv[c] = tinv[c] + z[0:ch]
                        lp[c] = z[ch:]
                    else:
                        tinv[c] = tinv[c] + _dot(tinv[c].astype(BF16), lp_bd)
                yield
            for c in cs:
                tb = tinv[c].astype(BF16)
                t_a = _dot(tb, stack(a_t[c]))
                tlv = _dot(tb, stack(lv[c]))
                ta[c] = jnp.concatenate([jnp.zeros_like(t_a), t_a], axis=0).astype(BF16)
                vt[c] = jnp.concatenate([v[sl[c]], tlv], axis=0).T
            yield

        state = [st_ref[si]]

        def advance(cs):
            for c in cs:
                g_in[c] = state[0]
                vu_t[c] = _dot_nt(g_in[c].astype(BF16), ta[c]) + vt[c]
                yield
                state[0] = state[0] * dec[c] + jnp.where(same_head, _dot(vu_t[c].astype(BF16), kb[c]), 0.0)
                yield

        def finish(cs):
            for c in cs:
                y_g = _dot(r_t[c].astype(BF16), g_in[c].T.astype(BF16))
                yield
                ys[c] = y_g + arkv[c] + _dot(a_rb[c], stack(vu_t[c][:, ch:].T))
                yield

        def finalize():
            st_ref[si] = state[0]
            y = jnp.concatenate([ys[c] for c in range(n_chunks)], axis=0)
            inv_hd = 1.0 / hd
            yc = y - _group_sums(y, ones_bd) * inv_hd
            var = _group_sums(yc * yc, ones_bd) * inv_hd
            yn = yc * lax.rsqrt(var + GN_EPS) * gng_ref[:, ln] + gnb_ref[:, ln]
            bonus = _group_sums(r * kp * rk_ref[:, ln], ones_bd) * v
            o_ref[:, ln] = ((yn + bonus) * g_ref[:, ln]).astype(o_ref.dtype)

        return prepare, advance, finish, finalize

    parts = [stream(si) for si in range(streams)]
    groups = [list(range(i, min(i + RWKV_LOOKAHEAD, n_chunks))) for i in range(0, n_chunks, RWKV_LOOKAHEAD)]

    def lockstep(gens):
        alive = False
        for g in gens:
            try:
                next(g)
                alive = True
            except StopIteration:
                pass
        return alive

    def drain(gens):
        while lockstep(gens):
            pass

    drain([p[0](groups[0]) for p in parts])
    for gi, grp in enumerate(groups):
        nxt = [p[0](groups[gi + 1]) for p in parts] if gi + 1 < len(groups) else []
        prv = [p[2](groups[gi - 1]) for p in parts] if gi > 0 else []
        adv = [p[1](grp) for p in parts]
        while lockstep(adv):
            lockstep(nxt)
            lockstep(prv)
        drain(nxt)
        drain(prv)
    drain([p[2](groups[-1]) for p in parts])
    for p in parts:
        p[3]()


def _rwkv(r, k, v, lw, a, g, k_k, k_a, r_k, gn_g, gn_b, *, batch, seq):
    n, d = r.shape
    group = RWKV_GROUP * RWKV_HEAD
    width = RWKV_STREAMS * group
    rows = min(RWKV_ROWS, seq)
    steps = seq // rows
    blk = pl.BlockSpec((rows, width), lambda b, h, t: (b * steps + t, h))
    par = pl.BlockSpec((1, width), lambda b, h, t: (0, h))
    return pl.pallas_call(
        functools.partial(_rwkv_kernel, rows=rows),
        grid=(batch, d // width, steps),
        in_specs=[blk] * 6 + [par] * 5,
        out_specs=blk,
        out_shape=jax.ShapeDtypeStruct((n, d), BF16),
        scratch_shapes=[pltpu.VMEM((RWKV_STREAMS, group, group), F32)],
        compiler_params=_cparams("parallel", "parallel", "arbitrary"),
        name="rwkv7",
    )(r, k, v, lw, a, g, k_k, k_a, r_k, gn_g, gn_b)


def _proj_norm_kernel(x_ref, z_ref, w_ref, g_ref, b_ref, o_ref, *, alpha):
    y = _dot(z_ref[...].astype(BF16), w_ref[...])
    o_ref[...] = _layer_norm(alpha * x_ref[...] + y, g_ref[...], b_ref[...])


def _proj_norm(x, z, w, ln_g, ln_b, *, seq, alpha):
    n, d = x.shape
    tm = min(ROW_TILE, seq)
    return pl.pallas_call(
        functools.partial(_proj_norm_kernel, alpha=alpha),
        grid=(n // tm,),
        in_specs=[pl.BlockSpec((tm, d), lambda i: (i, 0)), pl.BlockSpec((tm, d), lambda i: (i, 0)),
                  w.spec(), _const_spec((1, d)), _const_spec((1, d))],
        out_specs=pl.BlockSpec((tm, d), lambda i: (i, 0)),
        out_shape=jax.ShapeDtypeStruct((n, d), F32),
        compiler_params=_cparams("parallel"),
        name="rw_out_proj",
    )(x, z, w.stack, ln_g, ln_b)


def _pad_cols(w, mult):
    pad = (-w.shape[-1]) % mult
    return jnp.pad(w, ((0, 0),) * (w.ndim - 1) + ((0, pad),))


def _pad_rows(w, mult):
    pad = (-w.shape[-2]) % mult
    return jnp.pad(w, ((0, 0),) * (w.ndim - 2) + ((0, pad), (0, 0)))


def kernel(x, ln_mix_g, ln_mix_b, ln_ffn_g, ln_ffn_b, ev_w_in, ev_b_in, ev_conv_w, ev_conv_b, ev_cln_g, ev_cln_b, ev_lb_logits, ev_onorm_g, ev_w_out, rw_mu, rw_w_r, rw_w_k, rw_w_v, rw_w_o, rw_w0, rw_w1, rw_w2, rw_a0, rw_a1, rw_a2, rw_g1, rw_g2, rw_k_k, rw_k_a, rw_r_k, rw_gn_g, rw_gn_b, ff_w_up, ff_w_gate, ff_conv_w, ff_conv_b, ff_w_down):
    batch, seq, d = x.shape
    depth = ln_mix_g.shape[0]
    alpha = (2 * depth) ** 0.25
    row = lambda p: p.reshape(1, -1)
    bf = lambda w: w.astype(BF16)
    ev_w_in, ev_w_out = bf(ev_w_in), bf(ev_w_out)
    rw_w_r, rw_w_k, rw_w_v, rw_w_o = bf(rw_w_r), bf(rw_w_k), bf(rw_w_v), bf(rw_w_o)
    rw_w1, rw_a1, rw_g1 = (bf(_pad_cols(w, LANES)) for w in (rw_w1, rw_a1, rw_g1))
    rw_w2, rw_a2, rw_g2 = (bf(_pad_rows(w, LANES)) for w in (rw_w2, rw_a2, rw_g2))
    ff_w_up, ff_w_gate, ff_w_down = bf(ff_w_up), bf(ff_w_gate), bf(ff_w_down)
    h = x.reshape(batch * seq, d)
    for layer in range(depth):
        j = layer // 2
        if layer % 2 == 0:
            glu, q, lf, k, v, sg = _ev_in(h, _Layer(ev_w_in, j), row(ev_b_in[j]), ev_lb_logits, layer_j=j, seq=seq)
            ob = _hgrn(q, lf, k, v, sg, row(ev_onorm_g[j]), batch=batch, seq=seq)
            h = _ev_out(h, glu, ob, ev_conv_w[j], row(ev_conv_b[j]), row(ev_cln_g[j]), row(ev_cln_b[j]),
                        _Layer(ev_w_out, j), row(ln_mix_g[layer]), row(ln_mix_b[layer]), seq=seq, alpha=alpha)
        else:
            r, k, v, lw, a, g = _rw_in(
                h, rw_mu[j], _Layer(rw_w_r, j), _Layer(rw_w_k, j), _Layer(rw_w_v, j), row(rw_w0[j]),
                _Layer(rw_w1, j), _Layer(rw_w2, j), row(rw_a0[j]), _Layer(rw_a1, j), _Layer(rw_a2, j),
                _Layer(rw_g1, j), _Layer(rw_g2, j), seq=seq)
            z = _rwkv(r, k, v, lw, a, g, row(rw_k_k[j]), row(rw_k_a[j]), row(rw_r_k[j]), row(rw_gn_g[j]),
                      row(rw_gn_b[j]), batch=batch, seq=seq)
            h = _proj_norm(h, z, _Layer(rw_w_o, j), row(ln_mix_g[layer]), row(ln_mix_b[layer]), seq=seq,
                           alpha=alpha)
        h = _ffn(h, _Layer(ff_w_up, layer), _Layer(ff_w_gate, layer), ff_conv_w[layer], row(ff_conv_b[layer]),
                 _Layer(ff_w_down, layer), row(ln_ffn_g[layer]), row(ln_ffn_b[layer]), seq=seq, alpha=alpha)
    return h.reshape(batch, seq, d)
```
